```python
import jax, jax.numpy as jnp
from jax import lax
import numpy as np

D_MODEL = 2048
BATCH = 16
SEQ = 2048
DEPTH = 2
DEC_BATCH = 8
DEC_SEQ = 16
PAST_LEN = 1024

CHUNK = 64
Q_BLOCK = 64
HEAD_DIM = 128
ATT_WIDTH = D_MODEL // 2
CONV_WIDTH = D_MODEL // 4
POOL_WIDTH = D_MODEL // 4
N_HEADS = ATT_WIDTH // HEAD_DIM
N_KV_HEADS = 2
KV_WIDTH = N_KV_HEADS * HEAD_DIM
N_IDX_HEADS = 16
IDX_DIM = 64
TOPK_MAX = 256
ROPE_THETA = 10000.0
CONV_K = 31
POOL_WINDOWS = (2, 4, 8, 16)
N_POOL_GROUPS = 4
POOL_GROUP = POOL_WIDTH // N_POOL_GROUPS
POOL_HIST = 15
PLE_DIM = 256
EPS = 1e-6

_SPLITS = (("q", ATT_WIDTH), ("k", KV_WIDTH), ("v", KV_WIDTH), ("qi", N_IDX_HEADS * IDX_DIM),
           ("ki", IDX_DIM), ("wi", N_IDX_HEADS), ("g_att", ATT_WIDTH), ("glu", 2 * CONV_WIDTH),
           ("g_conv", CONV_WIDTH), ("pool_in", POOL_WIDTH), ("g_pool", POOL_WIDTH))
N_IN = 2 * ATT_WIDTH + 2 * KV_WIDTH + N_IDX_HEADS * IDX_DIM + IDX_DIM + N_IDX_HEADS + 3 * CONV_WIDTH + 2 * POOL_WIDTH

kernel_name = "hybrid_stream_dsa_conv_pool"


def _split_cols(z):
    out, off = {}, 0
    for name, w in _SPLITS:
        out[name] = z[..., off:off + w]
        off += w
    return out


def _rmsnorm(x, g):
    xf = x.astype(jnp.float32)
    y = xf * lax.rsqrt(jnp.mean(xf * xf, axis=-1, keepdims=True) + EPS) * g.astype(jnp.float32)
    return y.astype(x.dtype)


def _layernorm(x, g, b):
    xf = x.astype(jnp.float32)
    mu = jnp.mean(xf, axis=-1, keepdims=True)
    var = jnp.mean(jnp.square(xf - mu), axis=-1, keepdims=True)
    y = (xf - mu) * lax.rsqrt(var + EPS) * g.astype(jnp.float32) + b.astype(jnp.float32)
    return y.astype(x.dtype)


def _rope(x, pos):
    half = x.shape[-1] // 2
    freq = ROPE_THETA ** (-jnp.arange(half, dtype=jnp.float32) / half)
    ang = pos.astype(jnp.float32)[:, None] * freq[None, :]
    cos, sin = jnp.cos(ang)[:, None, :], jnp.sin(ang)[:, None, :]
    x1, x2 = x[..., :half].astype(jnp.float32), x[..., half:].astype(jnp.float32)
    return jnp.concatenate([x1 * cos - x2 * sin, x1 * sin + x2 * cos], axis=-1).astype(x.dtype)


def _dsa_block(q, qi, wi, k, v, ki, q_pos, k_pos, topk):
    B, Q = q.shape[:2]
    visible = (k_pos[None, :] // CHUNK) <= (q_pos[:, None] // CHUNK)
    logits = jnp.einsum('bqhd,bsd->bqhs', qi, ki, preferred_element_type=jnp.float32) * IDX_DIM ** -0.5
    score = jnp.einsum('bqhs,bqh->bqs', jax.nn.relu(logits), wi.astype(jnp.float32)) * N_IDX_HEADS ** -0.5
    score = jnp.where(visible[None], score, -jnp.inf)
    _, idx = lax.top_k(score, topk)
    sel_ok = (k_pos[idx] // CHUNK) <= (q_pos[None, :, None] // CHUNK)
    gather = jax.vmap(lambda a, i: a[i])
    kg, vg = gather(k, idx), gather(v, idx)
    qg = q.reshape(B, Q, N_KV_HEADS, N_HEADS // N_KV_HEADS, HEAD_DIM)
    s = jnp.einsum('bqgrd,bqkgd->bqgrk', qg, kg, preferred_element_type=jnp.float32) * HEAD_DIM ** -0.5
    s = jnp.where(sel_ok[:, :, None, None, :], s, -jnp.inf)
    pr = jax.nn.softmax(s, axis=-1).astype(v.dtype)
    o = jnp.einsum('bqgrk,bqkgd->bqgrd', pr, vg)
    return o.reshape(B, Q, ATT_WIDTH)


def _dsa_attention(q, qi, wi, k, v, ki, q_pos, k_pos, topk):
    B, T = q.shape[:2]
    if T <= Q_BLOCK:
        return _dsa_block(q, qi, wi, k, v, ki, q_pos, k_pos, topk)
    nb = T // Q_BLOCK
    blk = lambda a: jnp.moveaxis(a.reshape((B, nb, Q_BLOCK) + a.shape[2:]), 1, 0)
    out = lax.map(lambda a: _dsa_block(a[0], a[1], a[2], k, v, ki, a[3], k_pos, topk),
                  (blk(q), blk(qi), blk(wi), q_pos.reshape(nb, Q_BLOCK)))
    return jnp.moveaxis(out, 0, 1).reshape(B, T, ATT_WIDTH)


def _conformer_conv(u, prev, conv_w, conv_b, ln_g, ln_b, pw):
    xp = jnp.concatenate([prev.astype(u.dtype), u], axis=1)
    y = lax.conv_general_dilated(xp, conv_w.astype(u.dtype)[:, None, :], (1,), 'VALID',
                                 dimension_numbers=('NWC', 'WIO', 'NWC'), feature_group_count=u.shape[-1])
    y = jax.nn.silu(_layernorm(y + conv_b, ln_g, ln_b))
    return y @ pw, xp[:, -(CONV_K - 1):]


def _multiscale_pool(u, prev, pos, pool_w, pool_scale):
    B, T, C = u.shape
    xp = jnp.concatenate([prev.astype(u.dtype), u], axis=1)
    cs = jnp.cumsum(xp.astype(jnp.float32), axis=1)
    cs = jnp.concatenate([jnp.zeros((B, 1, C), jnp.float32), cs], axis=1)
    end = cs[:, POOL_HIST + 1:POOL_HIST + 1 + T]
    means = []
    for gi, w in enumerate(POOL_WINDOWS):
        sl = slice(gi * POOL_GROUP, (gi + 1) * POOL_GROUP)
        win = end[..., sl] - cs[:, POOL_HIST + 1 - w:POOL_HIST + 1 - w + T, sl]
        cnt = jnp.minimum(pos + 1, w).astype(jnp.float32)[:, None]
        means.append(win / cnt)
    r = (jnp.concatenate(means, axis=-1) - u.astype(jnp.float32)).astype(u.dtype)
    r = r.reshape(B, T, N_POOL_GROUPS, POOL_GROUP)
    y = jnp.einsum('btgc,gcd->btgd', r, pool_w).reshape(B, T, C) * pool_scale
    return y, xp[:, -POOL_HIST:]


def _layer(x, p, q_pos, k_past, v_past, ki_past, conv_prev, pool_prev, topk,
           norm_mix, w_in, conv_w, conv_b, conv_ln_g, conv_ln_b, conv_pw, pool_w, pool_scale,
           w_out, norm_ple, w_ple_gate, w_ple_proj):
    B, T, _ = x.shape
    z = _split_cols(_rmsnorm(x, norm_mix) @ w_in)
    q = _rope(z['q'].reshape(B, T, N_HEADS, HEAD_DIM), q_pos)
    k = _rope(z['k'].reshape(B, T, N_KV_HEADS, HEAD_DIM), q_pos)
    v = z['v'].reshape(B, T, N_KV_HEADS, HEAD_DIM)
    qi = _rope(z['qi'].reshape(B, T, N_IDX_HEADS, IDX_DIM), q_pos)
    ki = _rope(z['ki'].reshape(B, T, 1, IDX_DIM), q_pos)[:, :, 0]
    if k_past is None:
        k_all, v_all, ki_all, k_pos = k, v, ki, q_pos
    else:
        k_all = jnp.concatenate([k_past.astype(k.dtype), k], axis=1)
        v_all = jnp.concatenate([v_past.astype(v.dtype), v], axis=1)
        ki_all = jnp.concatenate([ki_past.astype(ki.dtype), ki], axis=1)
        k_pos = jnp.arange(k_all.shape[1])
    o_att = _dsa_attention(q, qi, z['wi'], k_all, v_all, ki_all, q_pos, k_pos, topk) * jax.nn.silu(z['g_att'])
    u = z['glu'][..., :CONV_WIDTH] * jax.nn.sigmoid(z['glu'][..., CONV_WIDTH:])
    o_conv, conv_state = _conformer_conv(u, conv_prev, conv_w, conv_b, conv_ln_g, conv_ln_b, conv_pw)
    o_conv = o_conv * jax.nn.silu(z['g_conv'])
    o_pool, pool_state = _multiscale_pool(z['pool_in'], pool_prev, q_pos, pool_w, pool_scale)
    o_pool = o_pool * jax.nn.silu(z['g_pool'])
    x = x + jnp.concatenate([o_att, o_conv, o_pool], axis=-1) @ w_out
    x = x + jax.nn.sigmoid(_rmsnorm(x, norm_ple) @ w_ple_gate) * (p @ w_ple_proj)
    return x, k, v, ki, conv_state, pool_state


def setup_inputs(seed: int = 0) -> dict:
    key = jax.random.key(seed)
    ks = jax.random.split(key, 24)
    nrm = lambda k, shape, s: jax.random.normal(k, shape, jnp.float32) * s
    return {
        "x_prompt": nrm(ks[0], (BATCH, SEQ, D_MODEL), 1.0),
        "x_sample": nrm(ks[1], (DEC_BATCH, DEC_SEQ, D_MODEL), 1.0),
        "p_prompt": nrm(ks[2], (DEPTH, BATCH, SEQ, PLE_DIM), 1.0),
        "p_sample": nrm(ks[3], (DEPTH, DEC_BATCH, DEC_SEQ, PLE_DIM), 1.0),
        "cache_k": nrm(ks[4], (DEPTH, DEC_BATCH, PAST_LEN, N_KV_HEADS, HEAD_DIM), 1.0),
        "cache_v": nrm(ks[5], (DEPTH, DEC_BATCH, PAST_LEN, N_KV_HEADS, HEAD_DIM), 1.0),
        "cache_kidx": nrm(ks[6], (DEPTH, DEC_BATCH, PAST_LEN, IDX_DIM), 1.0),
        "state_conv": nrm(ks[7], (DEPTH, DEC_BATCH, CONV_K - 1, CONV_WIDTH), 0.5),
        "state_pool": nrm(ks[8], (DEPTH, DEC_BATCH, POOL_HIST, POOL_WIDTH), 1.0),
        "norm_mix": 1.0 + nrm(ks[9], (DEPTH, D_MODEL), 0.05),
        "w_in": nrm(ks[10], (DEPTH, D_MODEL, N_IN), D_MODEL ** -0.5),
        "conv_w": nrm(ks[11], (DEPTH, CONV_K, CONV_WIDTH), CONV_K ** -0.5),
        "conv_b": nrm(ks[12], (DEPTH, CONV_WIDTH), 0.02),
        "conv_ln_g": 1.0 + nrm(ks[13], (DEPTH, CONV_WIDTH), 0.05),
        "conv_ln_b": nrm(ks[14], (DEPTH, CONV_WIDTH), 0.02),
        "conv_pw": nrm(ks[15], (DEPTH, CONV_WIDTH, CONV_WIDTH), CONV_WIDTH ** -0.5),
        "pool_w": nrm(ks[16], (DEPTH, N_POOL_GROUPS, POOL_GROUP, POOL_GROUP), POOL_GROUP ** -0.5),
        "pool_scale": 1.0 + nrm(ks[17], (DEPTH, POOL_WIDTH), 0.1),
        "w_out": nrm(ks[18], (DEPTH, D_MODEL, D_MODEL), D_MODEL ** -0.5),
        "norm_ple": 1.0 + nrm(ks[19], (DEPTH, D_MODEL), 0.05),
        "w_ple_gate": nrm(ks[20], (DEPTH, D_MODEL, D_MODEL), D_MODEL ** -0.5),
        "w_ple_proj": nrm(ks[21], (DEPTH, PLE_DIM, D_MODEL), PLE_DIM ** -0.5),
        "norm_final": 1.0 + nrm(ks[22], (D_MODEL,), 0.05),
    }


def reference(x_prompt, x_sample, p_prompt, p_sample, cache_k, cache_v, cache_kidx, state_conv, state_pool,
              norm_mix, w_in, conv_w, conv_b, conv_ln_g, conv_ln_b, conv_pw, pool_w, pool_scale,
              w_out, norm_ple, w_ple_gate, w_ple_proj, norm_final):
    Bp, Tp, _ = x_prompt.shape
    Bs, Ts, _ = x_sample.shape
    past = cache_k.shape[2]
    topk_p = min(TOPK_MAX, Tp // 4)
    topk_s = min(TOPK_MAX, (past + Ts) // 4)
    pos_p = jnp.arange(Tp)
    pos_s = past + jnp.arange(Ts)
    conv0 = jnp.zeros((Bp, CONV_K - 1, CONV_WIDTH), x_prompt.dtype)
    pool0 = jnp.zeros((Bp, POOL_HIST, POOL_WIDTH), x_prompt.dtype)
    hp, hs = x_prompt, x_sample
    kp, vp, kip, cp, pp = [], [], [], [], []
    kss, vss, kis, cs_, ps = [], [], [], [], []
    for i in range(DEPTH):
        lw = (norm_mix[i], w_in[i], conv_w[i], conv_b[i], conv_ln_g[i], conv_ln_b[i], conv_pw[i],
              pool_w[i], pool_scale[i], w_out[i], norm_ple[i], w_ple_gate[i], w_ple_proj[i])
        hp, k1, v1, ki1, c1, q1 = _layer(hp, p_prompt[i], pos_p, None, None, None, conv0, pool0, topk_p, *lw)
        hs, k2, v2, ki2, c2, q2 = _layer(hs, p_sample[i], pos_s, cache_k[i], cache_v[i], cache_kidx[i],
                                         state_conv[i], state_pool[i], topk_s, *lw)
        kp.append(k1); vp.append(v1); kip.append(ki1); cp.append(c1); pp.append(q1)
        kss.append(k2); vss.append(v2); kis.append(ki2); cs_.append(c2); ps.append(q2)
    y_prompt = _rmsnorm(hp, norm_final)
    y_sample = _rmsnorm(hs, norm_final)
    return (y_prompt, y_sample,
            jnp.stack(kp), jnp.stack(vp), jnp.stack(kip), jnp.stack(cp), jnp.stack(pp),
            jnp.stack(kss), jnp.stack(vss), jnp.stack(kis), jnp.stack(cs_), jnp.stack(ps))
```

```python
import functools

import numpy as np
import jax
import jax.numpy as jnp
from jax import lax
from jax.experimental import pallas as pl
from jax.experimental.pallas import tpu as pltpu

D_MODEL = 2048
CHUNK = 64
HEAD_DIM = 128
ATT_WIDTH = D_MODEL // 2
CONV_WIDTH = D_MODEL // 4
POOL_WIDTH = D_MODEL // 4
N_HEADS = ATT_WIDTH // HEAD_DIM
N_KV_HEADS = 2
HEADS_PER_KV = N_HEADS // N_KV_HEADS
KV_WIDTH = N_KV_HEADS * HEAD_DIM
N_IDX_HEADS = 16
IDX_DIM = 64
IDX_WIDTH = N_IDX_HEADS * IDX_DIM
TOPK_MAX = 256
ROPE_THETA = 10000.0
CONV_K = 31
CONV_HALO = 32
POOL_WINDOWS = (2, 4, 8, 16)
POOL_GROUP = POOL_WIDTH // len(POOL_WINDOWS)
POOL_HIST = 15
POOL_HALO = 16
PLE_DIM = 256
EPS = 1e-6
LANES = 128
NEG_BIG = -1e30
VMEM_LIMIT = 60 * 1024 * 1024

_OFF_Q = 0
_OFF_K = _OFF_Q + ATT_WIDTH
_OFF_V = _OFF_K + KV_WIDTH
_OFF_QI = _OFF_V + KV_WIDTH
_OFF_KI = _OFF_QI + IDX_WIDTH
_OFF_WI = _OFF_KI + IDX_DIM
_OFF_GATT = _OFF_WI + N_IDX_HEADS
_OFF_GLU = _OFF_GATT + ATT_WIDTH
_OFF_GCONV = _OFF_GLU + 2 * CONV_WIDTH
_OFF_POOL = _OFF_GCONV + CONV_WIDTH
_OFF_GPOOL = _OFF_POOL + POOL_WIDTH
N_IN = _OFF_GPOOL + POOL_WIDTH

_F32 = jnp.float32
_BF16 = jnp.bfloat16


def _const_spec(shape):
    nd = len(shape)
    return pl.BlockSpec(shape, lambda *_: (0,) * nd, pipeline_mode=pl.Buffered(1))


def _dot(a, b):
    return jnp.dot(a, b, preferred_element_type=_F32)


def _dot_nt(a, b):
    return lax.dot_general(a, b, (((1,), (1,)), ((), ())), preferred_element_type=_F32)


def _rms(x, g):
    return x * lax.rsqrt(jnp.mean(x * x, axis=-1, keepdims=True) + EPS) * g


def _inproj_kernel(x_ref, g_ref, c128_ref, s128_ref, c64_ref, s64a_ref, s64b_ref,
                   wqk_ref, wv_ref, widx_ref, wgate_ref, wglu_ref, wpool_ref,
                   q_ref, k_ref, v_ref, kb_ref, vb_ref, qi_ref, ki_ref, kib_ref, wi_ref,
                   gates_ref, u_ref, pin_ref):
    xn = _rms(x_ref[...], g_ref[...]).astype(_BF16)

    zqk = _dot(xn, wqk_ref[...])
    c128 = c128_ref[...]
    s128 = s128_ref[...]
    for h in range(N_HEADS + N_KV_HEADS):
        xh = zqk[:, h * HEAD_DIM:(h + 1) * HEAD_DIM]
        r = xh * c128 + pltpu.roll(xh, HEAD_DIM // 2, 1) * s128
        if h < N_HEADS:
            q_ref[:, h * HEAD_DIM:(h + 1) * HEAD_DIM] = (r * (HEAD_DIM ** -0.5)).astype(_BF16)
        else:
            j = h - N_HEADS
            k_ref[:, j * HEAD_DIM:(j + 1) * HEAD_DIM] = r
            kb_ref[:, j * HEAD_DIM:(j + 1) * HEAD_DIM] = r.astype(_BF16)

    zv = _dot(xn, wv_ref[...])
    v_ref[...] = zv
    vb_ref[...] = zv.astype(_BF16)

    zi = _dot(xn, widx_ref[...])
    c64 = c64_ref[...]
    s64a = s64a_ref[...]
    s64b = s64b_ref[...]
    n_groups = IDX_WIDTH // LANES
    for j in range(n_groups + 1):
        xj = zi[:, j * LANES:(j + 1) * LANES]
        r = (xj * c64 + pltpu.roll(xj, LANES - IDX_DIM // 2, 1) * s64a
             + pltpu.roll(xj, IDX_DIM // 2, 1) * s64b)
        if j < n_groups:
            qi_ref[:, j * LANES:(j + 1) * LANES] = r.astype(_BF16)
        else:
            ki_ref[...] = r[:, :IDX_DIM]
            kib_ref[...] = r[:, :IDX_DIM].astype(_BF16)
            wi_ref[...] = xj[:, IDX_DIM:IDX_DIM + N_IDX_HEADS] * (IDX_DIM ** -0.5 * N_IDX_HEADS ** -0.5)

    zg = _dot(xn, wgate_ref[...])
    gates_ref[...] = (zg * jax.nn.sigmoid(zg)).astype(_BF16)

    zu = _dot(xn, wglu_ref[...])
    u_ref[...] = zu[:, :CONV_WIDTH] * jax.nn.sigmoid(zu[:, CONV_WIDTH:])

    pin_ref[...] = _dot(xn, wpool_ref[...])


def _inproj(x2, norm, tabs, ws, *, tm, tiles_per_seq):
    n = x2.shape[0]
    grid = (n // tm,)
    row = lambda w: pl.BlockSpec((tm, w), lambda i: (i, 0))
    tab = pl.BlockSpec((tm, LANES), lambda i: (i % tiles_per_seq, 0))
    out_shape = [
        jax.ShapeDtypeStruct((n, ATT_WIDTH), _BF16),
        jax.ShapeDtypeStruct((n, KV_WIDTH), _F32),
        jax.ShapeDtypeStruct((n, KV_WIDTH), _F32),
        jax.ShapeDtypeStruct((n, KV_WIDTH), _BF16),
        jax.ShapeDtypeStruct((n, KV_WIDTH), _BF16),
        jax.ShapeDtypeStruct((n, IDX_WIDTH), _BF16),
        jax.ShapeDtypeStruct((n, IDX_DIM), _F32),
        jax.ShapeDtypeStruct((n, IDX_DIM), _BF16),
        jax.ShapeDtypeStruct((n, N_IDX_HEADS), _F32),
        jax.ShapeDtypeStruct((n, D_MODEL), _BF16),
        jax.ShapeDtypeStruct((n, CONV_WIDTH), _F32),
        jax.ShapeDtypeStruct((n, POOL_WIDTH), _F32),
    ]
    out_specs = [row(s.shape[1]) for s in out_shape]
    in_specs = ([row(D_MODEL), _const_spec((1, D_MODEL))] + [tab] * 5
                + [_const_spec(w.shape) for w in ws])
    return pl.pallas_call(
        _inproj_kernel, grid=grid, in_specs=in_specs, out_specs=out_specs, out_shape=out_shape,
        compiler_params=pltpu.CompilerParams(dimension_semantics=("arbitrary",),
                                             vmem_limit_bytes=VMEM_LIMIT),
        name="inproj",
    )(x2, norm, *tabs, *ws)


def _attn_kernel(q_ref, qi_ref, wi_ref, gate_ref, kb_ref, vb_ref, kib_ref, o_ref,
                 score_s, bias_s, *, tq, s_len, q_pos0, n_valid, topk):
    qt = pl.program_id(1)
    qi = qi_ref[...]
    kib = kib_ref[...]
    wi = wi_ref[...]

    acc = jnp.zeros((tq, s_len), _F32)
    for h in range(N_IDX_HEADS):
        l = _dot_nt(qi[:, h * IDX_DIM:(h + 1) * IDX_DIM], kib)
        acc = acc + jnp.maximum(l, 0.0) * wi[:, h:h + 1]

    qpos = q_pos0 + qt * tq + lax.broadcasted_iota(jnp.int32, (tq, 1), 0)
    kpos = lax.broadcasted_iota(jnp.int32, (1, s_len), 1)
    qchunk = lax.shift_right_logical(qpos, 6)
    vis = (lax.shift_right_logical(kpos, 6) <= qchunk) & (kpos < n_valid)
    n_vis = jnp.minimum((qchunk + 1) * CHUNK, n_valid)
    score_s[...] = jnp.where(vis, acc, -jnp.inf)

    int_min = jnp.int32(-2 ** 31)

    def decode(u):
        sk = u ^ int_min
        bits = jnp.where(sk < 0, sk ^ jnp.int32(0x7FFFFFFF), sk)
        return lax.bitcast_convert_type(bits, _F32)

    def search(i, u):
        cand = u | lax.shift_left(jnp.int32(1), 31 - i)
        cnt = jnp.sum((score_s[...] >= decode(cand)).astype(_F32), axis=-1, keepdims=True)
        return jnp.where(cnt >= topk, cand, u)

    u = lax.fori_loop(0, 32, search, jnp.zeros((tq, 1), jnp.int32))
    thr = jnp.where(n_vis <= topk, -jnp.inf, decode(u))

    score = score_s[...]
    ge = score >= thr
    n_ge = jnp.sum(ge.astype(_F32), axis=-1, keepdims=True)
    bias_s[...] = jnp.where(ge & vis, 0.0, NEG_BIG)

    has_tie = jnp.max(jnp.where((n_ge > topk) & (n_vis > topk), 1.0, 0.0)) > 0.0

    @pl.when(has_tie)
    def _():
        sc = score_s[...]
        gt = sc > thr
        eq = (sc == thr) & vis
        need = topk - jnp.sum(gt.astype(_F32), axis=-1, keepdims=True)
        before = (lax.broadcasted_iota(jnp.int32, (LANES, LANES), 0)
                  < lax.broadcasted_iota(jnp.int32, (LANES, LANES), 1)).astype(_BF16)
        run = jnp.zeros((tq, 1), _F32)
        for kb in range(s_len // LANES):
            sl = slice(kb * LANES, (kb + 1) * LANES)
            e = eq[:, sl].astype(_F32)
            rank = _dot(e.astype(_BF16), before) + run
            pick = (gt[:, sl] & vis[:, sl]) | (eq[:, sl] & (rank < need))
            bias_s[:, sl] = jnp.where(pick, 0.0, NEG_BIG)
            run = run + jnp.sum(e, axis=-1, keepdims=True)

    q = q_ref[...]
    gate = gate_ref[...].astype(_F32)
    bias = bias_s[...]
    for g in range(N_KV_HEADS):
        heads = [g * HEADS_PER_KV + r for r in range(HEADS_PER_KV)]
        qg = jnp.concatenate([q[:, h * HEAD_DIM:(h + 1) * HEAD_DIM] for h in heads], axis=0)
        s = _dot_nt(qg, kb_ref[:, g * HEAD_DIM:(g + 1) * HEAD_DIM])
        s = (s.reshape(HEADS_PER_KV, tq, s_len) + bias[None]).reshape(HEADS_PER_KV * tq, s_len)
        m = jnp.max(s, axis=-1, keepdims=True)
        p = jnp.exp(s - m)
        den = jnp.sum(p, axis=-1, keepdims=True)
        o = _dot(p.astype(_BF16), vb_ref[:, g * HEAD_DIM:(g + 1) * HEAD_DIM]) / den
        for r, h in enumerate(heads):
            cols = slice(h * HEAD_DIM, (h + 1) * HEAD_DIM)
            o_ref[:, cols] = (o[r * tq:(r + 1) * tq] * gate[:, cols]).astype(_BF16)


def _attention(q, qi, wi, gates, kb, vb, kib, *, tq, q_pos0, n_valid, topk):
    b, t, _ = q.shape
    s_len = kb.shape[1]
    grid = (b, t // tq)
    qspec = lambda w: pl.BlockSpec((None, tq, w), lambda i, j: (i, j, 0))
    kspec = lambda w: pl.BlockSpec((None, s_len, w), lambda i, j: (i, 0, 0))
    kern = functools.partial(_attn_kernel, tq=tq, s_len=s_len, q_pos0=q_pos0, n_valid=n_valid, topk=topk)
    return pl.pallas_call(
        kern, grid=grid,
        in_specs=[qspec(ATT_WIDTH), qspec(IDX_WIDTH), qspec(N_IDX_HEADS), qspec(ATT_WIDTH),
                  kspec(KV_WIDTH), kspec(KV_WIDTH), kspec(IDX_DIM)],
        out_specs=qspec(ATT_WIDTH),
        out_shape=jax.ShapeDtypeStruct((b, t, ATT_WIDTH), _BF16),
        scratch_shapes=[pltpu.VMEM((tq, s_len), _F32), pltpu.VMEM((tq, s_len), _F32)],
        compiler_params=pltpu.CompilerParams(dimension_semantics=("arbitrary", "arbitrary"),
                                             vmem_limit_bytes=VMEM_LIMIT),
        name="attention",
    )(q, qi, wi, gates, kb, vb, kib)


def _mix_kernel(u_ref, uprev_ref, ust_ref, pin_ref, pprev_ref, pst_ref, gconv_ref, gpool_ref,
                cw_ref, cb_ref, lng_ref, lnb_ref, pw_ref, poolw_ref, pscale_ref,
                oc_ref, op_ref, xp_s, pp_s, *, tt, q_pos0):
    t = pl.program_id(1)

    @pl.when(t == 0)
    def _():
        xp_s[0:CONV_HALO, :] = ust_ref[...]
        pp_s[0:POOL_HALO, :] = pst_ref[...]

    @pl.when(t > 0)
    def _():
        xp_s[0:CONV_HALO, :] = uprev_ref[...]
        pp_s[0:POOL_HALO, :] = pprev_ref[...]

    xp_s[CONV_HALO:CONV_HALO + tt, :] = u_ref[...]
    pin = pin_ref[...]
    pp_s[POOL_HALO:POOL_HALO + tt, :] = pin

    first = CONV_HALO - (CONV_K - 1)
    acc = jnp.zeros((tt, CONV_WIDTH), _F32)
    for j in range(CONV_K):
        acc = acc + xp_s[pl.ds(first + j, tt), :] * cw_ref[j:j + 1, :]
    y = acc + cb_ref[...]
    mu = jnp.mean(y, axis=-1, keepdims=True)
    yc = y - mu
    var = jnp.mean(yc * yc, axis=-1, keepdims=True)
    yn = yc * lax.rsqrt(var + EPS) * lng_ref[...] + lnb_ref[...]
    act = (yn * jax.nn.sigmoid(yn)).astype(_BF16)
    oc_ref[...] = (_dot(act, pw_ref[...]) * gconv_ref[...].astype(_F32)).astype(_BF16)

    pos = q_pos0 + t * tt + lax.broadcasted_iota(jnp.int32, (tt, 1), 0)
    for gi, w in enumerate(POOL_WINDOWS):
        cols = slice(gi * POOL_GROUP, (gi + 1) * POOL_GROUP)
        win = pp_s[pl.ds(POOL_HALO, tt), cols]
        for i in range(1, w):
            win = win + pp_s[pl.ds(POOL_HALO - i, tt), cols]
        cnt = jnp.minimum(pos + 1, w).astype(_F32)
        r = (win / cnt - pin[:, cols]).astype(_BF16)
        yp = _dot(r, poolw_ref[gi]) * pscale_ref[:, cols]
        op_ref[:, cols] = (yp * gpool_ref[:, cols].astype(_F32)).astype(_BF16)


def _mixers(u, ust, pin, pst, gates, cw, cb, lng, lnb, pw, poolw, pscale, *, tt, q_pos0):
    b, t, _ = u.shape
    grid = (b, t // tt)
    cur = lambda w: pl.BlockSpec((None, tt, w), lambda i, j: (i, j, 0))

    def prev(rows, src_rows):
        if src_rows < rows:
            return None
        per = tt // rows
        return pl.BlockSpec((None, rows, CONV_WIDTH), lambda i, j: (i, jnp.maximum(j * per - 1, 0), 0))

    uprev_spec = prev(CONV_HALO, t)
    pprev_spec = prev(POOL_HALO, t)
    state = lambda rows: pl.BlockSpec((None, rows, CONV_WIDTH), lambda i, j: (i, 0, 0))
    uprev_src, pprev_src = u, pin
    if uprev_spec is None:
        assert t == tt
        uprev_src, uprev_spec = ust, state(CONV_HALO)
    if pprev_spec is None:
        assert t == tt
        pprev_src, pprev_spec = pst, state(POOL_HALO)
    gate_col = lambda blk: pl.BlockSpec((None, tt, CONV_WIDTH), lambda i, j: (i, j, blk))
    kern = functools.partial(_mix_kernel, tt=tt, q_pos0=q_pos0)
    return pl.pallas_call(
        kern, grid=grid,
        in_specs=[cur(CONV_WIDTH), uprev_spec, state(CONV_HALO), cur(POOL_WIDTH), pprev_spec, state(POOL_HALO),
                  gate_col(ATT_WIDTH // CONV_WIDTH), gate_col(ATT_WIDTH // CONV_WIDTH + 1),
                  _const_spec(cw.shape), _const_spec(cb.shape), _const_spec(lng.shape), _const_spec(lnb.shape),
                  _const_spec(pw.shape), _const_spec(poolw.shape), _const_spec(pscale.shape)],
        out_specs=[cur(CONV_WIDTH), cur(POOL_WIDTH)],
        out_shape=[jax.ShapeDtypeStruct((b, t, CONV_WIDTH), _BF16), jax.ShapeDtypeStruct((b, t, POOL_WIDTH), _BF16)],
        scratch_shapes=[pltpu.VMEM((CONV_HALO + tt, CONV_WIDTH), _F32), pltpu.VMEM((POOL_HALO + tt, POOL_WIDTH), _F32)],
        compiler_params=pltpu.CompilerParams(dimension_semantics=("arbitrary", "arbitrary"),
                                             vmem_limit_bytes=VMEM_LIMIT),
        name="mixers",
    )(u, uprev_src, ust, pin, pprev_src, pst, gates, gates, cw, cb, lng, lnb, pw, poolw, pscale)


def _outproj_kernel(x_ref, a_ref, c_ref, pl_ref, p_ref, woa_ref, woc_ref, wop_ref, nple_ref,
                    wgate_ref, wproj_ref, nfin_ref, o_ref, *, final):
    x1 = (x_ref[...] + _dot(a_ref[...], woa_ref[...]) + _dot(c_ref[...], woc_ref[...])
          + _dot(pl_ref[...], wop_ref[...]))
    xn = _rms(x1, nple_ref[...]).astype(_BF16)
    gate = jax.nn.sigmoid(_dot(xn, wgate_ref[...]))
    x2 = x1 + gate * _dot(p_ref[...].astype(_BF16), wproj_ref[...])
    o_ref[...] = _rms(x2, nfin_ref[...]) if final else x2


def _outproj(x2, a, c, po, p, w_out, nple, wgate, wproj, nfin, *, tm, final):
    n = x2.shape[0]
    row = lambda w: pl.BlockSpec((tm, w), lambda i: (i, 0))
    wblk = lambda rows, blk: pl.BlockSpec((rows, D_MODEL), lambda i: (blk, 0), pipeline_mode=pl.Buffered(1))
    return pl.pallas_call(
        functools.partial(_outproj_kernel, final=final), grid=(n // tm,),
        in_specs=[row(D_MODEL), row(ATT_WIDTH), row(CONV_WIDTH), row(POOL_WIDTH), row(PLE_DIM),
                  wblk(ATT_WIDTH, 0), wblk(CONV_WIDTH, ATT_WIDTH // CONV_WIDTH),
                  wblk(POOL_WIDTH, ATT_WIDTH // CONV_WIDTH + 1),
                  _const_spec(nple.shape), _const_spec(wgate.shape), _const_spec(wproj.shape),
                  _const_spec(nfin.shape)],
        out_specs=row(D_MODEL),
        out_shape=jax.ShapeDtypeStruct((n, D_MODEL), _F32),
        compiler_params=pltpu.CompilerParams(dimension_semantics=("arbitrary",),
                                             vmem_limit_bytes=VMEM_LIMIT),
        name="outproj",
    )(x2, a, c, po, p, w_out, w_out, w_out, nple, wgate, wproj, nfin)


def _rope_tables(pos):
    pos = pos.astype(_F32)[:, None]

    def cs(half):
        freq = ROPE_THETA ** (-jnp.arange(half, dtype=_F32) / half)
        ang = pos * freq[None, :]
        return jnp.cos(ang), jnp.sin(ang)

    c, s = cs(HEAD_DIM // 2)
    c128 = jnp.concatenate([c, c], axis=-1)
    s128 = jnp.concatenate([-s, s], axis=-1)
    c, s = cs(IDX_DIM // 2)
    z = jnp.zeros_like(s)
    c64 = jnp.concatenate([c, c, c, c], axis=-1)
    s64a = jnp.concatenate([-s, z, -s, z], axis=-1)
    s64b = jnp.concatenate([z, s, z, s], axis=-1)
    return c128, s128, c64, s64a, s64b


def _split_w_in(w):
    w = w.astype(_BF16)
    cols = lambda a, n: w[:, a:a + n]
    pad = jnp.zeros((w.shape[0], LANES - IDX_DIM - N_IDX_HEADS), _BF16)
    wqk = jnp.concatenate([cols(_OFF_Q, ATT_WIDTH), cols(_OFF_K, KV_WIDTH)], axis=1)
    wv = cols(_OFF_V, KV_WIDTH)
    widx = jnp.concatenate([cols(_OFF_QI, IDX_WIDTH), cols(_OFF_KI, IDX_DIM), cols(_OFF_WI, N_IDX_HEADS), pad], axis=1)
    wgate = jnp.concatenate([cols(_OFF_GATT, ATT_WIDTH), cols(_OFF_GCONV, CONV_WIDTH), cols(_OFF_GPOOL, POOL_WIDTH)], axis=1)
    wglu = cols(_OFF_GLU, 2 * CONV_WIDTH)
    wpool = cols(_OFF_POOL, POOL_WIDTH)
    return wqk, wv, widx, wgate, wglu, wpool


def _row_tile(n, pref):
    t = min(pref, n)
    while n % t:
        t //= 2
    return t


def _layer(x, p, pos0, k_past, v_past, ki_past, conv_prev, pool_prev, topk, lw, nfin, final):
    b, t, _ = x.shape
    n = b * t
    (norm_mix, ws, cw, cb, lng, lnb, pw, poolw, pscale, w_out, nple, wgate, wproj) = lw

    tm = _row_tile(t, 256) if b > 1 and t >= 256 else n
    if tm <= t:
        tabs = _rope_tables(pos0 + jnp.arange(t))
        tiles_per_seq = t // tm
    else:
        tabs = tuple(jnp.tile(a, (b, 1)) for a in _rope_tables(pos0 + jnp.arange(t)))
        tiles_per_seq = 1
    (q, k, v, kb, vb, qi, ki, kib, wi, gates, u, pin) = _inproj(
        x.reshape(n, D_MODEL), norm_mix, tabs, ws, tm=tm, tiles_per_seq=tiles_per_seq)
    r3 = lambda a: a.reshape(b, t, a.shape[-1])

    kb3, vb3, kib3 = r3(kb), r3(vb), r3(kib)
    if k_past is not None:
        past = k_past.shape[1]
        kb3 = jnp.concatenate([k_past.reshape(b, past, KV_WIDTH).astype(_BF16), kb3], axis=1)
        vb3 = jnp.concatenate([v_past.reshape(b, past, KV_WIDTH).astype(_BF16), vb3], axis=1)
        kib3 = jnp.concatenate([ki_past.astype(_BF16), kib3], axis=1)
    n_valid = kb3.shape[1]
    s_pad = -(-n_valid // LANES) * LANES
    if s_pad != n_valid:
        padk = lambda a: jnp.pad(a, ((0, 0), (0, s_pad - n_valid), (0, 0)))
        kb3, vb3, kib3 = padk(kb3), padk(vb3), padk(kib3)
    tq = _row_tile(t, 128)
    o_att = _attention(r3(q), r3(qi), r3(wi), r3(gates), kb3, vb3, kib3,
                       tq=tq, q_pos0=pos0, n_valid=n_valid, topk=topk)

    u3, pin3 = r3(u), r3(pin)
    ust = jnp.pad(conv_prev, ((0, 0), (CONV_HALO - (CONV_K - 1), 0), (0, 0)))
    pst = jnp.pad(pool_prev, ((0, 0), (POOL_HALO - POOL_HIST, 0), (0, 0)))
    tt = _row_tile(t, 256)
    oc, op = _mixers(u3, ust, pin3, pst, r3(gates), cw, cb, lng, lnb, pw, poolw, pscale, tt=tt, q_pos0=pos0)

    tm2 = _row_tile(n, 256)
    x_new = _outproj(x.reshape(n, D_MODEL), o_att.reshape(n, ATT_WIDTH), oc.reshape(n, CONV_WIDTH),
                     op.reshape(n, POOL_WIDTH), p.reshape(n, PLE_DIM), w_out, nple, wgate, wproj, nfin,
                     tm=tm2, final=final).reshape(b, t, D_MODEL)

    conv_state = jnp.concatenate([conv_prev, u3], axis=1)[:, -(CONV_K - 1):]
    pool_state = jnp.concatenate([pool_prev, pin3], axis=1)[:, -POOL_HIST:]
    return (x_new, k.reshape(b, t, N_KV_HEADS, HEAD_DIM), v.reshape(b, t, N_KV_HEADS, HEAD_DIM),
            r3(ki), conv_state, pool_state)


def kernel(x_prompt, x_sample, p_prompt, p_sample, cache_k, cache_v, cache_kidx, state_conv, state_pool, norm_mix, w_in, conv_w, conv_b, conv_ln_g, conv_ln_b, conv_pw, pool_w, pool_scale, w_out, norm_ple, w_ple_gate, w_ple_proj, norm_final):
    bp, tp, _ = x_prompt.shape
    bs, ts, _ = x_sample.shape
    depth = w_in.shape[0]
    past = cache_k.shape[2]
    topk_p = min(TOPK_MAX, tp // 4)
    topk_s = min(TOPK_MAX, (past + ts) // 4)
    conv0 = jnp.zeros((bp, CONV_K - 1, CONV_WIDTH), x_prompt.dtype)
    pool0 = jnp.zeros((bp, POOL_HIST, POOL_WIDTH), x_prompt.dtype)
    nfin = norm_final.reshape(1, D_MODEL)
    hp, hs = x_prompt, x_sample
    outs_p, outs_s = [], []
    for i in range(depth):
        vec = lambda a: a[i].reshape(1, -1)
        lw = (vec(norm_mix), _split_w_in(w_in[i]), conv_w[i], vec(conv_b), vec(conv_ln_g), vec(conv_ln_b),
              conv_pw[i].astype(_BF16), pool_w[i].astype(_BF16), vec(pool_scale), w_out[i].astype(_BF16),
              vec(norm_ple), w_ple_gate[i].astype(_BF16), w_ple_proj[i].astype(_BF16))
        final = i == depth - 1
        hp, *op_ = _layer(hp, p_prompt[i], 0, None, None, None, conv0, pool0, topk_p, lw, nfin, final)
        hs, *os_ = _layer(hs, p_sample[i], past, cache_k[i], cache_v[i], cache_kidx[i],
                          state_conv[i], state_pool[i], topk_s, lw, nfin, final)
        outs_p.append(op_)
        outs_s.append(os_)
    stack = lambda outs, j: jnp.stack([o[j] for o in outs])
    return ((hp, hs) + tuple(stack(outs_p, j) for j in range(5)) + tuple(stack(outs_s, j) for j in range(5)))
```

```python
import functools

import numpy as np
import jax
import jax.numpy as jnp
from jax import lax
from jax.experimental import pallas as pl
from jax.experimental.pallas import tpu as pltpu

D_MODEL = 2048
CHUNK = 64
HEAD_DIM = 128
ATT_WIDTH = D_MODEL // 2
CONV_WIDTH = D_MODEL // 4
POOL_WIDTH = D_MODEL // 4
N_HEADS = ATT_WIDTH // HEAD_DIM
N_KV_HEADS = 2
HEADS_PER_KV = N_HEADS // N_KV_HEADS
KV_WIDTH = N_KV_HEADS * HEAD_DIM
N_IDX_HEADS = 16
IDX_DIM = 64
IDX_WIDTH = N_IDX_HEADS * IDX_DIM
TOPK_MAX = 256
ROPE_THETA = 10000.0
CONV_K = 31
CONV_HALO = 32
POOL_WINDOWS = (2, 4, 8, 16)
POOL_GROUP = POOL_WIDTH // len(POOL_WINDOWS)
POOL_HIST = 15
POOL_HALO = 16
PLE_DIM = 256
EPS = 1e-6
LANES = 128
NEG_BIG = -1e30
VMEM_LIMIT = 60 * 1024 * 1024
ATTN_TQ = 256
ATTN_SUB = 128

_OFF_Q = 0
_OFF_K = _OFF_Q + ATT_WIDTH
_OFF_V = _OFF_K + KV_WIDTH
_OFF_QI = _OFF_V + KV_WIDTH
_OFF_KI = _OFF_QI + IDX_WIDTH
_OFF_WI = _OFF_KI + IDX_DIM
_OFF_GATT = _OFF_WI + N_IDX_HEADS
_OFF_GLU = _OFF_GATT + ATT_WIDTH
_OFF_GCONV = _OFF_GLU + 2 * CONV_WIDTH
_OFF_POOL = _OFF_GCONV + CONV_WIDTH
_OFF_GPOOL = _OFF_POOL + POOL_WIDTH
N_IN = _OFF_GPOOL + POOL_WIDTH

_F32 = jnp.float32
_BF16 = jnp.bfloat16


def _const_spec(shape):
    nd = len(shape)
    return pl.BlockSpec(shape, lambda *_: (0,) * nd, pipeline_mode=pl.Buffered(1))


def _dot(a, b):
    return jnp.dot(a, b, preferred_element_type=_F32)


def _dot_nt(a, b):
    return lax.dot_general(a, b, (((1,), (1,)), ((), ())), preferred_element_type=_F32)


def _rms(x, g):
    return x * lax.rsqrt(jnp.mean(x * x, axis=-1, keepdims=True) + EPS) * g


def _inproj_kernel(x_ref, g_ref, c128_ref, s128_ref, c64_ref, s64a_ref, s64b_ref,
                   wqk_ref, wv_ref, widx_ref, wgate_ref, wglu_ref, wpool_ref,
                   q_ref, k_ref, v_ref, kb_ref, vb_ref, qi_ref, ki_ref, kib_ref, wi_ref,
                   gates_ref, u_ref, pin_ref):
    xn = _rms(x_ref[...], g_ref[...]).astype(_BF16)

    zqk = _dot(xn, wqk_ref[...])
    c128 = c128_ref[...]
    s128 = s128_ref[...]
    for h in range(N_HEADS + N_KV_HEADS):
        xh = zqk[:, h * HEAD_DIM:(h + 1) * HEAD_DIM]
        r = xh * c128 + pltpu.roll(xh, HEAD_DIM // 2, 1) * s128
        if h < N_HEADS:
            q_ref[:, h * HEAD_DIM:(h + 1) * HEAD_DIM] = (r * (HEAD_DIM ** -0.5)).astype(_BF16)
        else:
            j = h - N_HEADS
            k_ref[:, j * HEAD_DIM:(j + 1) * HEAD_DIM] = r
            kb_ref[:, j * HEAD_DIM:(j + 1) * HEAD_DIM] = r.astype(_BF16)

    zv = _dot(xn, wv_ref[...])
    v_ref[...] = zv
    vb_ref[...] = zv.astype(_BF16)

    zi = _dot(xn, widx_ref[...])
    c64 = c64_ref[...]
    s64a = s64a_ref[...]
    s64b = s64b_ref[...]
    n_groups = IDX_WIDTH // LANES
    for j in range(n_groups + 1):
        xj = zi[:, j * LANES:(j + 1) * LANES]
        r = (xj * c64 + pltpu.roll(xj, LANES - IDX_DIM // 2, 1) * s64a
             + pltpu.roll(xj, IDX_DIM // 2, 1) * s64b)
        if j < n_groups:
            qi_ref[:, j * LANES:(j + 1) * LANES] = r.astype(_BF16)
        else:
            ki_ref[...] = r[:, :IDX_DIM]
            kib_ref[...] = r[:, :IDX_DIM].astype(_BF16)
            wi_ref[...] = xj[:, IDX_DIM:IDX_DIM + N_IDX_HEADS] * (IDX_DIM ** -0.5 * N_IDX_HEADS ** -0.5)

    zg = _dot(xn, wgate_ref[...])
    gates_ref[...] = (zg * jax.nn.sigmoid(zg)).astype(_BF16)

    zu = _dot(xn, wglu_ref[...])
    u_ref[...] = zu[:, :CONV_WIDTH] * jax.nn.sigmoid(zu[:, CONV_WIDTH:])

    pin_ref[...] = _dot(xn, wpool_ref[...])


def _inproj(x2, norm, tabs, ws, *, tm, tiles_per_seq):
    n = x2.shape[0]
    grid = (n // tm,)
    row = lambda w: pl.BlockSpec((tm, w), lambda i: (i, 0))
    tab = pl.BlockSpec((tm, LANES), lambda i: (i % tiles_per_seq, 0))
    out_shape = [
        jax.ShapeDtypeStruct((n, ATT_WIDTH), _BF16),
        jax.ShapeDtypeStruct((n, KV_WIDTH), _F32),
        jax.ShapeDtypeStruct((n, KV_WIDTH), _F32),
        jax.ShapeDtypeStruct((n, KV_WIDTH), _BF16),
        jax.ShapeDtypeStruct((n, KV_WIDTH), _BF16),
        jax.ShapeDtypeStruct((n, IDX_WIDTH), _BF16),
        jax.ShapeDtypeStruct((n, IDX_DIM), _F32),
        jax.ShapeDtypeStruct((n, IDX_DIM), _BF16),
        jax.ShapeDtypeStruct((n, N_IDX_HEADS), _F32),
        jax.ShapeDtypeStruct((n, D_MODEL), _BF16),
        jax.ShapeDtypeStruct((n, CONV_WIDTH), _F32),
        jax.ShapeDtypeStruct((n, POOL_WIDTH), _F32),
    ]
    out_specs = [row(s.shape[1]) for s in out_shape]
    in_specs = ([row(D_MODEL), _const_spec((1, D_MODEL))] + [tab] * 5
                + [_const_spec(w.shape) for w in ws])
    return pl.pallas_call(
        _inproj_kernel, grid=grid, in_specs=in_specs, out_specs=out_specs, out_shape=out_shape,
        compiler_params=pltpu.CompilerParams(dimension_semantics=("arbitrary",),
                                             vmem_limit_bytes=VMEM_LIMIT),
        name="inproj",
    )(x2, norm, *tabs, *ws)


def _attn_kernel(q_ref, gate_ref, kb_ref, vb_ref, qi0_ref, wi0_ref, kib0_ref, qi1_ref, wi1_ref, kib1_ref,
                 obuf_ref, o_ref, score_s, bias_s, qi_s, w_s, *, tq, sub, s_len, q_pos0, n_valid, topk):
    del obuf_ref
    b = pl.program_id(0)
    slot = lax.rem(b, 2)
    nslot = 1 - slot
    half = s_len // 2
    nblk = half // LANES
    n_slices = 2 * N_IDX_HEADS
    assert n_slices == 32

    qpos = q_pos0 + lax.broadcasted_iota(jnp.int32, (tq, 1), 0)
    qchunk = lax.div(qpos, jnp.int32(CHUNK))
    n_vis = jnp.minimum((qchunk + 1) * CHUNK, n_valid)

    def vis_half(hf):
        kpos = hf * half + lax.broadcasted_iota(jnp.int32, (1, half), 1)
        return (lax.div(kpos, jnp.int32(CHUNK)) <= qchunk) & (kpos < n_valid)

    if s_len > topk:
        @pl.when(b == 0)
        def _():
            qi = qi0_ref[...]
            wi = wi0_ref[...]
            for hf in range(2):
                kib = kib0_ref[hf * half:(hf + 1) * half, :]
                acc = jnp.zeros((tq, half), _F32)
                for h in range(N_IDX_HEADS):
                    l = _dot_nt(qi[:, h * IDX_DIM:(h + 1) * IDX_DIM], kib)
                    acc = acc + jnp.maximum(l, 0.0) * wi[:, h:h + 1]
                score_s[0, hf] = jnp.where(vis_half(hf), acc, -jnp.inf)

        qi1 = qi1_ref[...]
        wi1 = wi1_ref[...]
        for h in range(N_IDX_HEADS):
            qi_s[h] = qi1[:, h * IDX_DIM:(h + 1) * IDX_DIM]
            w_s[h] = jnp.broadcast_to(wi1[:, h:h + 1], (tq, LANES))
        score_s[nslot] = jnp.zeros((2, tq, half), _F32)

        int_min = jnp.int32(-2 ** 31)

        def decode(u):
            sk = u ^ int_min
            bits = jnp.where(sk < 0, sk ^ jnp.int32(0x7FFFFFFF), sk)
            return lax.bitcast_convert_type(bits, _F32)

        def count_ge(t):
            c0 = jnp.sum((score_s[slot, 0] >= t).astype(_F32), axis=-1, keepdims=True)
            c1 = jnp.sum((score_s[slot, 1] >= t).astype(_F32), axis=-1, keepdims=True)
            return c0 + c1

        def step(i, u):
            cand = u | lax.shift_left(jnp.int32(1), 31 - i)
            u = jnp.where(count_ge(decode(cand)) >= topk, cand, u)
            h = lax.div(i, jnp.int32(2))
            hf = lax.rem(i, jnp.int32(2))
            l = _dot_nt(qi_s[h], kib1_ref[pl.ds(pl.multiple_of(hf * half, LANES), half), :])
            wt = w_s[h]
            t = jnp.concatenate([jnp.maximum(l[:, j * LANES:(j + 1) * LANES], 0.0) * wt for j in range(nblk)],
                                axis=1)
            score_s[nslot, hf] = score_s[nslot, hf] + t
            return u

        u = lax.fori_loop(0, n_slices, step, jnp.zeros((tq, 1), jnp.int32))
        for hf in range(2):
            score_s[nslot, hf] = jnp.where(vis_half(hf), score_s[nslot, hf], -jnp.inf)
        thr = jnp.where(n_vis <= topk, -jnp.inf, decode(u))

        n_ge = jnp.zeros((tq, 1), _F32)
        for hf in range(2):
            ge = score_s[slot, hf] >= thr
            n_ge = n_ge + jnp.sum(ge.astype(_F32), axis=-1, keepdims=True)
            bias_s[hf] = jnp.where(ge & vis_half(hf), 0.0, NEG_BIG)

        has_tie = jnp.max(jnp.where((n_ge > topk) & (n_vis > topk), 1.0, 0.0)) > 0.0

        @pl.when(has_tie)
        def _():
            n_gt = jnp.zeros((tq, 1), _F32)
            for hf in range(2):
                n_gt = n_gt + jnp.sum((score_s[slot, hf] > thr).astype(_F32), axis=-1, keepdims=True)
            need = topk - n_gt
            before = (lax.broadcasted_iota(jnp.int32, (LANES, LANES), 0)
                      < lax.broadcasted_iota(jnp.int32, (LANES, LANES), 1)).astype(_BF16)
            run = jnp.zeros((tq, 1), _F32)
            for hf in range(2):
                sc = score_s[slot, hf]
                vis = vis_half(hf)
                gt = (sc > thr) & vis
                eq = (sc == thr) & vis
                for j in range(nblk):
                    sl = slice(j * LANES, (j + 1) * LANES)
                    e = eq[:, sl].astype(_F32)
                    rank = _dot(e.astype(_BF16), before) + run
                    pick = gt[:, sl] | (eq[:, sl] & (rank < need))
                    bias_s[hf, :, sl] = jnp.where(pick, 0.0, NEG_BIG)
                    run = run + jnp.sum(e, axis=-1, keepdims=True)
    else:
        for hf in range(2):
            bias_s[hf] = jnp.where(vis_half(hf), 0.0, NEG_BIG)

    for r0 in range(0, tq, sub):
        rows = slice(r0, r0 + sub)
        q = q_ref[rows, :]
        gate = gate_ref[rows, :].astype(_F32)
        bias = jnp.concatenate([bias_s[0, rows, :], bias_s[1, rows, :]], axis=1)
        for g in range(N_KV_HEADS):
            heads = [g * HEADS_PER_KV + r for r in range(HEADS_PER_KV)]
            qg = jnp.concatenate([q[:, h * HEAD_DIM:(h + 1) * HEAD_DIM] for h in heads], axis=0)
            s = _dot_nt(qg, kb_ref[:, g * HEAD_DIM:(g + 1) * HEAD_DIM])
            s = (s.reshape(HEADS_PER_KV, sub, s_len) + bias[None]).reshape(HEADS_PER_KV * sub, s_len)
            m = jnp.max(s, axis=-1, keepdims=True)
            p = jnp.exp(s - m)
            den = jnp.sum(p, axis=-1, keepdims=True)
            o = _dot(p.astype(_BF16), vb_ref[:, g * HEAD_DIM:(g + 1) * HEAD_DIM]) / den
            for r, h in enumerate(heads):
                cols = slice(h * HEAD_DIM, (h + 1) * HEAD_DIM)
                o_ref[rows, cols] = (o[r * sub:(r + 1) * sub] * gate[:, cols]).astype(_BF16)


def _attention_tile(obuf, q, qi, wi, gates, kb, vb, kib, *, tq, qblk, s_len, q_pos0, n_valid, topk):
    b = q.shape[0]
    cur = lambda w: pl.BlockSpec((None, tq, w), lambda i: (i, qblk, 0))
    nxt = lambda w: pl.BlockSpec((None, tq, w), lambda i: (jnp.minimum(i + 1, b - 1), qblk, 0))
    kcur = lambda w: pl.BlockSpec((None, s_len, w), lambda i: (i, 0, 0))
    knxt = lambda w: pl.BlockSpec((None, s_len, w), lambda i: (jnp.minimum(i + 1, b - 1), 0, 0))
    kern = functools.partial(_attn_kernel, tq=tq, sub=min(tq, ATTN_SUB), s_len=s_len, q_pos0=q_pos0,
                             n_valid=n_valid, topk=topk)
    half = s_len // 2
    return pl.pallas_call(
        kern, grid=(b,),
        in_specs=[cur(ATT_WIDTH), cur(ATT_WIDTH), kcur(KV_WIDTH), kcur(KV_WIDTH),
                  cur(IDX_WIDTH), cur(N_IDX_HEADS), kcur(IDX_DIM),
                  nxt(IDX_WIDTH), nxt(N_IDX_HEADS), knxt(IDX_DIM),
                  pl.BlockSpec(memory_space=pl.ANY)],
        out_specs=cur(ATT_WIDTH),
        out_shape=jax.ShapeDtypeStruct(obuf.shape, _BF16),
        input_output_aliases={10: 0},
        scratch_shapes=[pltpu.VMEM((2, 2, tq, half), _F32), pltpu.VMEM((2, tq, half), _F32),
                        pltpu.VMEM((N_IDX_HEADS, tq, IDX_DIM), _BF16), pltpu.VMEM((N_IDX_HEADS, tq, LANES), _F32)],
        compiler_params=pltpu.CompilerParams(dimension_semantics=("arbitrary",),
                                             vmem_limit_bytes=VMEM_LIMIT),
        name="attention",
    )(q, gates, kb, vb, qi, wi, kib, qi, wi, kib, obuf)


def _attention(q, qi, wi, gates, kb, vb, kib, *, q_pos0, n_valid, topk, causal_tiles):
    b, t, _ = q.shape
    obuf = jnp.zeros((b, t, ATT_WIDTH), _BF16)
    if causal_tiles:
        tq = _row_tile(t, ATTN_TQ)
        assert q_pos0 == 0 and tq % (2 * LANES) == 0
        for j in range(t // tq):
            obuf = _attention_tile(obuf, q, qi, wi, gates, kb, vb, kib, tq=tq, qblk=j, s_len=(j + 1) * tq,
                                   q_pos0=j * tq, n_valid=n_valid, topk=topk)
        return obuf
    return _attention_tile(obuf, q, qi, wi, gates, kb, vb, kib, tq=t, qblk=0, s_len=kb.shape[1],
                           q_pos0=q_pos0, n_valid=n_valid, topk=topk)


def _mix_kernel(u_ref, uprev_ref, ust_ref, pin_ref, pprev_ref, pst_ref, gconv_ref, gpool_ref,
                cw_ref, cb_ref, lng_ref, lnb_ref, pw_ref, poolw_ref, pscale_ref,
                oc_ref, op_ref, xp_s, pp_s, *, tt, q_pos0):
    t = pl.program_id(1)

    @pl.when(t == 0)
    def _():
        xp_s[0:CONV_HALO, :] = ust_ref[...]
        pp_s[0:POOL_HALO, :] = pst_ref[...]

    @pl.when(t > 0)
    def _():
        xp_s[0:CONV_HALO, :] = uprev_ref[...]
        pp_s[0:POOL_HALO, :] = pprev_ref[...]

    xp_s[CONV_HALO:CONV_HALO + tt, :] = u_ref[...]
    pin = pin_ref[...]
    pp_s[POOL_HALO:POOL_HALO + tt, :] = pin

    first = CONV_HALO - (CONV_K - 1)
    acc = jnp.zeros((tt, CONV_WIDTH), _F32)
    for j in range(CONV_K):
        acc = acc + xp_s[pl.ds(first + j, tt), :] * cw_ref[j:j + 1, :]
    y = acc + cb_ref[...]
    mu = jnp.mean(y, axis=-1, keepdims=True)
    yc = y - mu
    var = jnp.mean(yc * yc, axis=-1, keepdims=True)
    yn = yc * lax.rsqrt(var + EPS) * lng_ref[...] + lnb_ref[...]
    act = (yn * jax.nn.sigmoid(yn)).astype(_BF16)
    oc_ref[...] = (_dot(act, pw_ref[...]) * gconv_ref[...].astype(_F32)).astype(_BF16)

    pos = q_pos0 + t * tt + lax.broadcasted_iota(jnp.int32, (tt, 1), 0)
    for gi, w in enumerate(POOL_WINDOWS):
        cols = slice(gi * POOL_GROUP, (gi + 1) * POOL_GROUP)
        win = pp_s[pl.ds(POOL_HALO, tt), cols]
        for i in range(1, w):
            win = win + pp_s[pl.ds(POOL_HALO - i, tt), cols]
        cnt = jnp.minimum(pos + 1, w).astype(_F32)
        r = (win / cnt - pin[:, cols]).astype(_BF16)
        yp = _dot(r, poolw_ref[gi]) * pscale_ref[:, cols]
        op_ref[:, cols] = (yp * gpool_ref[:, cols].astype(_F32)).astype(_BF16)


def _mixers(u, ust, pin, pst, gates, cw, cb, lng, lnb, pw, poolw, pscale, *, tt, q_pos0):
    b, t, _ = u.shape
    grid = (b, t // tt)
    cur = lambda w: pl.BlockSpec((None, tt, w), lambda i, j: (i, j, 0))

    def prev(rows, src_rows):
        if src_rows < rows:
            return None
        per = tt // rows
        return pl.BlockSpec((None, rows, CONV_WIDTH), lambda i, j: (i, jnp.maximum(j * per - 1, 0), 0))

    uprev_spec = prev(CONV_HALO, t)
    pprev_spec = prev(POOL_HALO, t)
    state = lambda rows: pl.BlockSpec((None, rows, CONV_WIDTH), lambda i, j: (i, 0, 0))
    uprev_src, pprev_src = u, pin
    if uprev_spec is None:
        assert t == tt
        uprev_src, uprev_spec = ust, state(CONV_HALO)
    if pprev_spec is None:
        assert t == tt
        pprev_src, pprev_spec = pst, state(POOL_HALO)
    gate_col = lambda blk: pl.BlockSpec((None, tt, CONV_WIDTH), lambda i, j: (i, j, blk))
    kern = functools.partial(_mix_kernel, tt=tt, q_pos0=q_pos0)
    return pl.pallas_call(
        kern, grid=grid,
        in_specs=[cur(CONV_WIDTH), uprev_spec, state(CONV_HALO), cur(POOL_WIDTH), pprev_spec, state(POOL_HALO),
                  gate_col(ATT_WIDTH // CONV_WIDTH), gate_col(ATT_WIDTH // CONV_WIDTH + 1),
                  _const_spec(cw.shape), _const_spec(cb.shape), _const_spec(lng.shape), _const_spec(lnb.shape),
                  _const_spec(pw.shape), _const_spec(poolw.shape), _const_spec(pscale.shape)],
        out_specs=[cur(CONV_WIDTH), cur(POOL_WIDTH)],
        out_shape=[jax.ShapeDtypeStruct((b, t, CONV_WIDTH), _BF16), jax.ShapeDtypeStruct((b, t, POOL_WIDTH), _BF16)],
        scratch_shapes=[pltpu.VMEM((CONV_HALO + tt, CONV_WIDTH), _F32), pltpu.VMEM((POOL_HALO + tt, POOL_WIDTH), _F32)],
        compiler_params=pltpu.CompilerParams(dimension_semantics=("arbitrary", "arbitrary"),
                                             vmem_limit_bytes=VMEM_LIMIT),
        name="mixers",
    )(u, uprev_src, ust, pin, pprev_src, pst, gates, gates, cw, cb, lng, lnb, pw, poolw, pscale)


def _outproj_kernel(x_ref, a_ref, c_ref, pl_ref, p_ref, woa_ref, woc_ref, wop_ref, nple_ref,
                    wgate_ref, wproj_ref, nfin_ref, o_ref, *, final):
    x1 = (x_ref[...] + _dot(a_ref[...], woa_ref[...]) + _dot(c_ref[...], woc_ref[...])
          + _dot(pl_ref[...], wop_ref[...]))
    xn = _rms(x1, nple_ref[...]).astype(_BF16)
    gate = jax.nn.sigmoid(_dot(xn, wgate_ref[...]))
    x2 = x1 + gate * _dot(p_ref[...].astype(_BF16), wproj_ref[...])
    o_ref[...] = _rms(x2, nfin_ref[...]) if final else x2


def _outproj(x2, a, c, po, p, w_out, nple, wgate, wproj, nfin, *, tm, final):
    n = x2.shape[0]
    row = lambda w: pl.BlockSpec((tm, w), lambda i: (i, 0))
    wblk = lambda rows, blk: pl.BlockSpec((rows, D_MODEL), lambda i: (blk, 0), pipeline_mode=pl.Buffered(1))
    return pl.pallas_call(
        functools.partial(_outproj_kernel, final=final), grid=(n // tm,),
        in_specs=[row(D_MODEL), row(ATT_WIDTH), row(CONV_WIDTH), row(POOL_WIDTH), row(PLE_DIM),
                  wblk(ATT_WIDTH, 0), wblk(CONV_WIDTH, ATT_WIDTH // CONV_WIDTH),
                  wblk(POOL_WIDTH, ATT_WIDTH // CONV_WIDTH + 1),
                  _const_spec(nple.shape), _const_spec(wgate.shape), _const_spec(wproj.shape),
                  _const_spec(nfin.shape)],
        out_specs=row(D_MODEL),
        out_shape=jax.ShapeDtypeStruct((n, D_MODEL), _F32),
        compiler_params=pltpu.CompilerParams(dimension_semantics=("arbitrary",),
                                             vmem_limit_bytes=VMEM_LIMIT),
        name="outproj",
    )(x2, a, c, po, p, w_out, w_out, w_out, nple, wgate, wproj, nfin)


def _rope_tables(pos):
    pos = pos.astype(_F32)[:, None]

    def cs(half):
        freq = ROPE_THETA ** (-jnp.arange(half, dtype=_F32) / half)
        ang = pos * freq[None, :]
        return jnp.cos(ang), jnp.sin(ang)

    c, s = cs(HEAD_DIM // 2)
    c128 = jnp.concatenate([c, c], axis=-1)
    s128 = jnp.concatenate([-s, s], axis=-1)
    c, s = cs(IDX_DIM // 2)
    z = jnp.zeros_like(s)
    c64 = jnp.concatenate([c, c, c, c], axis=-1)
    s64a = jnp.concatenate([-s, z, -s, z], axis=-1)
    s64b = jnp.concatenate([z, s, z, s], axis=-1)
    return c128, s128, c64, s64a, s64b


def _split_w_in(w):
    w = w.astype(_BF16)
    cols = lambda a, n: w[:, a:a + n]
    pad = jnp.zeros((w.shape[0], LANES - IDX_DIM - N_IDX_HEADS), _BF16)
    wqk = jnp.concatenate([cols(_OFF_Q, ATT_WIDTH), cols(_OFF_K, KV_WIDTH)], axis=1)
    wv = cols(_OFF_V, KV_WIDTH)
    widx = jnp.concatenate([cols(_OFF_QI, IDX_WIDTH), cols(_OFF_KI, IDX_DIM), cols(_OFF_WI, N_IDX_HEADS), pad], axis=1)
    wgate = jnp.concatenate([cols(_OFF_GATT, ATT_WIDTH), cols(_OFF_GCONV, CONV_WIDTH), cols(_OFF_GPOOL, POOL_WIDTH)], axis=1)
    wglu = cols(_OFF_GLU, 2 * CONV_WIDTH)
    wpool = cols(_OFF_POOL, POOL_WIDTH)
    return wqk, wv, widx, wgate, wglu, wpool


def _row_tile(n, pref):
    t = min(pref, n)
    while n % t:
        t //= 2
    return t


def _layer(x, p, pos0, k_past, v_past, ki_past, conv_prev, pool_prev, topk, lw, nfin, final):
    b, t, _ = x.shape
    n = b * t
    (norm_mix, ws, cw, cb, lng, lnb, pw, poolw, pscale, w_out, nple, wgate, wproj) = lw

    tm = _row_tile(t, 256) if b > 1 and t >= 256 else n
    if tm <= t:
        tabs = _rope_tables(pos0 + jnp.arange(t))
        tiles_per_seq = t // tm
    else:
        tabs = tuple(jnp.tile(a, (b, 1)) for a in _rope_tables(pos0 + jnp.arange(t)))
        tiles_per_seq = 1
    (q, k, v, kb, vb, qi, ki, kib, wi, gates, u, pin) = _inproj(
        x.reshape(n, D_MODEL), norm_mix, tabs, ws, tm=tm, tiles_per_seq=tiles_per_seq)
    r3 = lambda a: a.reshape(b, t, a.shape[-1])

    kb3, vb3, kib3 = r3(kb), r3(vb), r3(kib)
    if k_past is not None:
        past = k_past.shape[1]
        kb3 = jnp.concatenate([k_past.reshape(b, past, KV_WIDTH).astype(_BF16), kb3], axis=1)
        vb3 = jnp.concatenate([v_past.reshape(b, past, KV_WIDTH).astype(_BF16), vb3], axis=1)
        kib3 = jnp.concatenate([ki_past.astype(_BF16), kib3], axis=1)
    n_valid = kb3.shape[1]
    s_pad = -(-n_valid // (2 * LANES)) * (2 * LANES)
    if s_pad != n_valid:
        padk = lambda a: jnp.pad(a, ((0, 0), (0, s_pad - n_valid), (0, 0)))
        kb3, vb3, kib3 = padk(kb3), padk(vb3), padk(kib3)
    o_att = _attention(r3(q), r3(qi), r3(wi), r3(gates), kb3, vb3, kib3, q_pos0=pos0, n_valid=n_valid,
                       topk=topk, causal_tiles=k_past is None and t % (2 * LANES) == 0)

    u3, pin3 = r3(u), r3(pin)
    ust = jnp.pad(conv_prev, ((0, 0), (CONV_HALO - (CONV_K - 1), 0), (0, 0)))
    pst = jnp.pad(pool_prev, ((0, 0), (POOL_HALO - POOL_HIST, 0), (0, 0)))
    tt = _row_tile(t, 256)
    oc, op = _mixers(u3, ust, pin3, pst, r3(gates), cw, cb, lng, lnb, pw, poolw, pscale, tt=tt, q_pos0=pos0)

    tm2 = _row_tile(n, 256)
    x_new = _outproj(x.reshape(n, D_MODEL), o_att.reshape(n, ATT_WIDTH), oc.reshape(n, CONV_WIDTH),
                     op.reshape(n, POOL_WIDTH), p.reshape(n, PLE_DIM), w_out, nple, wgate, wproj, nfin,
                     tm=tm2, final=final).reshape(b, t, D_MODEL)

    conv_state = jnp.concatenate([conv_prev, u3], axis=1)[:, -(CONV_K - 1):]
    pool_state = jnp.concatenate([pool_prev, pin3], axis=1)[:, -POOL_HIST:]
    return (x_new, k.reshape(b, t, N_KV_HEADS, HEAD_DIM), v.reshape(b, t, N_KV_HEADS, HEAD_DIM),
            r3(ki), conv_state, pool_state)


def kernel(x_prompt, x_sample, p_prompt, p_sample, cache_k, cache_v, cache_kidx, state_conv, state_pool, norm_mix, w_in, conv_w, conv_b, conv_ln_g, conv_ln_b, conv_pw, pool_w, pool_scale, w_out, norm_ple, w_ple_gate, w_ple_proj, norm_final):
    bp, tp, _ = x_prompt.shape
    bs, ts, _ = x_sample.shape
    depth = w_in.shape[0]
    past = cache_k.shape[2]
    topk_p = min(TOPK_MAX, tp // 4)
    topk_s = min(TOPK_MAX, (past + ts) // 4)
    conv0 = jnp.zeros((bp, CONV_K - 1, CONV_WIDTH), x_prompt.dtype)
    pool0 = jnp.zeros((bp, POOL_HIST, POOL_WIDTH), x_prompt.dtype)
    nfin = norm_final.reshape(1, D_MODEL)
    hp, hs = x_prompt, x_sample
    outs_p, outs_s = [], []
    for i in range(depth):
        vec = lambda a: a[i].reshape(1, -1)
        lw = (vec(norm_mix), _split_w_in(w_in[i]), conv_w[i], vec(conv_b), vec(conv_ln_g), vec(conv_ln_b),
              conv_pw[i].astype(_BF16), pool_w[i].astype(_BF16), vec(pool_scale), w_out[i].astype(_BF16),
              vec(norm_ple), w_ple_gate[i].astype(_BF16), w_ple_proj[i].astype(_BF16))
        final = i == depth - 1
        hp, *op_ = _layer(hp, p_prompt[i], 0, None, None, None, conv0, pool0, topk_p, lw, nfin, final)
        hs, *os_ = _layer(hs, p_sample[i], past, cache_k[i], cache_v[i], cache_kidx[i],
                          state_conv[i], state_pool[i], topk_s, lw, nfin, final)
        outs_p.append(op_)
        outs_s.append(os_)
    stack = lambda outs, j: jnp.stack([o[j] for o in outs])
    return ((hp, hs) + tuple(stack(outs_p, j) for j in range(5)) + tuple(stack(outs_s, j) for j in range(5)))
```

```python
import functools

import numpy as np
import jax
import jax.numpy as jnp
from jax import lax
from jax.experimental import pallas as pl
from jax.experimental.pallas import tpu as pltpu

D_MODEL = 2048
CHUNK = 64
HEAD_DIM = 128
ATT_WIDTH = D_MODEL // 2
CONV_WIDTH = D_MODEL // 4
POOL_WIDTH = D_MODEL // 4
N_HEADS = ATT_WIDTH // HEAD_DIM
N_KV_HEADS = 2
HEADS_PER_KV = N_HEADS // N_KV_HEADS
KV_WIDTH = N_KV_HEADS * HEAD_DIM
N_IDX_HEADS = 16
IDX_DIM = 64
IDX_WIDTH = N_IDX_HEADS * IDX_DIM
TOPK_MAX = 256
ROPE_THETA = 10000.0
CONV_K = 31
CONV_HALO = 32
POOL_WINDOWS = (2, 4, 8, 16)
POOL_GROUP = POOL_WIDTH // len(POOL_WINDOWS)
POOL_HIST = 15
POOL_HALO = 16
PLE_DIM = 256
EPS = 1e-6
LANES = 128
SUBLANES = 8
NEG_BIG = -1e30
VMEM_LIMIT = 60 * 1024 * 1024
ATTN_TQ = 256
ATTN_SUB = 128

_OFF_Q = 0
_OFF_K = _OFF_Q + ATT_WIDTH
_OFF_V = _OFF_K + KV_WIDTH
_OFF_QI = _OFF_V + KV_WIDTH
_OFF_KI = _OFF_QI + IDX_WIDTH
_OFF_WI = _OFF_KI + IDX_DIM
_OFF_GATT = _OFF_WI + N_IDX_HEADS
_OFF_GLU = _OFF_GATT + ATT_WIDTH
_OFF_GCONV = _OFF_GLU + 2 * CONV_WIDTH
_OFF_POOL = _OFF_GCONV + CONV_WIDTH
_OFF_GPOOL = _OFF_POOL + POOL_WIDTH
N_IN = _OFF_GPOOL + POOL_WIDTH

_F32 = jnp.float32
_BF16 = jnp.bfloat16


def _const_spec(shape):
    nd = len(shape)
    return pl.BlockSpec(shape, lambda *_: (0,) * nd, pipeline_mode=pl.Buffered(1))


def _dot(a, b):
    return jnp.dot(a, b, preferred_element_type=_F32)


def _dot_nt(a, b):
    return lax.dot_general(a, b, (((1,), (1,)), ((), ())), preferred_element_type=_F32)


def _rms(x, g):
    return x * lax.rsqrt(jnp.mean(x * x, axis=-1, keepdims=True) + EPS) * g


def _inproj_kernel(x_ref, g_ref, c128_ref, s128_ref, c64_ref, s64a_ref, s64b_ref,
                   wqk_ref, wv_ref, widx_ref, wgate_ref, wglu_ref, wpool_ref,
                   q_ref, k_ref, v_ref, kb_ref, vb_ref, qi_ref, ki_ref, kib_ref, wi_ref,
                   gates_ref, u_ref, pin_ref):
    xn = _rms(x_ref[...], g_ref[...]).astype(_BF16)

    zqk = _dot(xn, wqk_ref[...])
    c128 = c128_ref[...]
    s128 = s128_ref[...]
    for h in range(N_HEADS + N_KV_HEADS):
        xh = zqk[:, h * HEAD_DIM:(h + 1) * HEAD_DIM]
        r = xh * c128 + pltpu.roll(xh, HEAD_DIM // 2, 1) * s128
        if h < N_HEADS:
            q_ref[:, h * HEAD_DIM:(h + 1) * HEAD_DIM] = (r * (HEAD_DIM ** -0.5)).astype(_BF16)
        else:
            j = h - N_HEADS
            k_ref[:, j * HEAD_DIM:(j + 1) * HEAD_DIM] = r
            kb_ref[:, j * HEAD_DIM:(j + 1) * HEAD_DIM] = r.astype(_BF16)

    zv = _dot(xn, wv_ref[...])
    v_ref[...] = zv
    vb_ref[...] = zv.astype(_BF16)

    zi = _dot(xn, widx_ref[...])
    c64 = c64_ref[...]
    s64a = s64a_ref[...]
    s64b = s64b_ref[...]
    n_groups = IDX_WIDTH // LANES
    for j in range(n_groups + 1):
        xj = zi[:, j * LANES:(j + 1) * LANES]
        r = (xj * c64 + pltpu.roll(xj, LANES - IDX_DIM // 2, 1) * s64a
             + pltpu.roll(xj, IDX_DIM // 2, 1) * s64b)
        if j < n_groups:
            qi_ref[:, j * LANES:(j + 1) * LANES] = r.astype(_BF16)
        else:
            ki_ref[...] = r[:, :IDX_DIM]
            kib_ref[...] = r[:, :IDX_DIM].astype(_BF16)
            wi_ref[...] = xj[:, IDX_DIM:IDX_DIM + N_IDX_HEADS] * (IDX_DIM ** -0.5 * N_IDX_HEADS ** -0.5)

    zg = _dot(xn, wgate_ref[...])
    gates_ref[...] = (zg * jax.nn.sigmoid(zg)).astype(_BF16)

    zu = _dot(xn, wglu_ref[...])
    u_ref[...] = zu[:, :CONV_WIDTH] * jax.nn.sigmoid(zu[:, CONV_WIDTH:])

    pin_ref[...] = _dot(xn, wpool_ref[...])


def _inproj(x2, norm, tabs, ws, *, tm, tiles_per_seq):
    n = x2.shape[0]
    grid = (n // tm,)
    row = lambda w: pl.BlockSpec((tm, w), lambda i: (i, 0))
    tab = pl.BlockSpec((tm, LANES), lambda i: (i % tiles_per_seq, 0))
    out_shape = [
        jax.ShapeDtypeStruct((n, ATT_WIDTH), _BF16),
        jax.ShapeDtypeStruct((n, KV_WIDTH), _F32),
        jax.ShapeDtypeStruct((n, KV_WIDTH), _F32),
        jax.ShapeDtypeStruct((n, KV_WIDTH), _BF16),
        jax.ShapeDtypeStruct((n, KV_WIDTH), _BF16),
        jax.ShapeDtypeStruct((n, IDX_WIDTH), _BF16),
        jax.ShapeDtypeStruct((n, IDX_DIM), _F32),
        jax.ShapeDtypeStruct((n, IDX_DIM), _BF16),
        jax.ShapeDtypeStruct((n, N_IDX_HEADS), _F32),
        jax.ShapeDtypeStruct((n, D_MODEL), _BF16),
        jax.ShapeDtypeStruct((n, CONV_WIDTH), _F32),
        jax.ShapeDtypeStruct((n, POOL_WIDTH), _F32),
    ]
    out_specs = [row(s.shape[1]) for s in out_shape]
    in_specs = ([row(D_MODEL), _const_spec((1, D_MODEL))] + [tab] * 5
                + [_const_spec(w.shape) for w in ws])
    return pl.pallas_call(
        _inproj_kernel, grid=grid, in_specs=in_specs, out_specs=out_specs, out_shape=out_shape,
        compiler_params=pltpu.CompilerParams(dimension_semantics=("arbitrary",),
                                             vmem_limit_bytes=VMEM_LIMIT),
        name="inproj",
    )(x2, norm, *tabs, *ws)


def _softmax_attention(q_ref, gate_ref, kb_ref, vb_ref, bias_s, o_ref, *, tq, sub, s_len):
    for r0 in range(0, tq, sub):
        rows = slice(r0, r0 + sub)
        q = q_ref[rows, :]
        gate = gate_ref[rows, :].astype(_F32)
        bias = bias_s[rows, :]
        for g in range(N_KV_HEADS):
            heads = [g * HEADS_PER_KV + r for r in range(HEADS_PER_KV)]
            qg = jnp.concatenate([q[:, h * HEAD_DIM:(h + 1) * HEAD_DIM] for h in heads], axis=0)
            s = _dot_nt(qg, kb_ref[:, g * HEAD_DIM:(g + 1) * HEAD_DIM])
            s = (s.reshape(HEADS_PER_KV, sub, s_len) + bias[None]).reshape(HEADS_PER_KV * sub, s_len)
            m = jnp.max(s, axis=-1, keepdims=True)
            p = jnp.exp(s - m)
            den = jnp.sum(p, axis=-1, keepdims=True)
            o = _dot(p.astype(_BF16), vb_ref[:, g * HEAD_DIM:(g + 1) * HEAD_DIM]) / den
            for r, h in enumerate(heads):
                cols = slice(h * HEAD_DIM, (h + 1) * HEAD_DIM)
                o_ref[rows, cols] = (o[r * sub:(r + 1) * sub] * gate[:, cols]).astype(_BF16)


def _visible(tq, cols, col0, q_pos0, n_valid):
    qpos = q_pos0 + lax.broadcasted_iota(jnp.int32, (tq, 1), 0)
    kpos = col0 + lax.broadcasted_iota(jnp.int32, (1, cols), 1)
    return (lax.div(kpos, jnp.int32(CHUNK)) <= lax.div(qpos, jnp.int32(CHUNK))) & (kpos < n_valid)


def _attn_dense_kernel(q_ref, gate_ref, kb_ref, vb_ref, o_ref, bias_s, *, tq, sub, s_len, q_pos0, n_valid):
    bias_s[...] = jnp.where(_visible(tq, s_len, 0, q_pos0, n_valid), 0.0, NEG_BIG)
    _softmax_attention(q_ref, gate_ref, kb_ref, vb_ref, bias_s, o_ref, tq=tq, sub=sub, s_len=s_len)


def _attn_kernel(q_ref, gate_ref, kb_ref, vb_ref, qi0_ref, wi0_ref, kib0_ref, qi1_ref, wi1_ref, kib1_ref,
                 obuf_ref, o_ref, score_s, acc_s, bias_s, qi_s, w_s, *, tq, sub, s_len, q_pos0, n_valid, topk):
    del obuf_ref
    b = pl.program_id(0)
    slot = lax.rem(b, 2)
    nslot = 1 - slot
    half = s_len // 2
    nblk = half // LANES
    n_slices = 2 * N_IDX_HEADS
    assert n_slices == 32

    qpos = q_pos0 + lax.broadcasted_iota(jnp.int32, (tq, 1), 0)
    n_vis = jnp.minimum((lax.div(qpos, jnp.int32(CHUNK)) + 1) * CHUNK, n_valid)
    vis_half = lambda hf: _visible(tq, half, hf * half, q_pos0, n_valid)

    @pl.when(b == 0)
    def _():
        qi = qi0_ref[...]
        wi = wi0_ref[...]
        for hf in range(2):
            kib = kib0_ref[hf * half:(hf + 1) * half, :]
            acc = jnp.zeros((tq, half), _F32)
            for h in range(N_IDX_HEADS):
                l = _dot_nt(qi[:, h * IDX_DIM:(h + 1) * IDX_DIM], kib)
                acc = acc + jnp.maximum(l, 0.0) * wi[:, h:h + 1]
            score_s[0, hf] = jnp.where(vis_half(hf), acc, -jnp.inf)

    qi1 = qi1_ref[...]
    wi1 = wi1_ref[...]
    for h in range(N_IDX_HEADS):
        qi_s[h] = qi1[:, h * IDX_DIM:(h + 1) * IDX_DIM]
        w_s[h] = jnp.broadcast_to(wi1[:, h:h + 1], (tq, LANES))
    acc_s[...] = jnp.zeros((2, tq, half), _F32)

    int_min = jnp.int32(-2 ** 31)

    def decode(u):
        sk = u ^ int_min
        bits = jnp.where(sk < 0, sk ^ jnp.int32(0x7FFFFFFF), sk)
        return lax.bitcast_convert_type(bits, _F32)

    def count(pred):
        return (jnp.sum(pred(score_s[slot, 0]).astype(_F32), axis=-1, keepdims=True)
                + jnp.sum(pred(score_s[slot, 1]).astype(_F32), axis=-1, keepdims=True))

    def step(i, u):
        cand = u | lax.shift_left(jnp.int32(1), 31 - i)
        t = decode(cand)
        u = jnp.where(count(lambda s: s >= t) >= topk, cand, u)
        h = lax.div(i, jnp.int32(2))
        hf = lax.rem(i, jnp.int32(2))
        l = _dot_nt(qi_s[h], kib1_ref[pl.ds(pl.multiple_of(hf * half, LANES), half), :])
        wt = w_s[h]
        term = jnp.concatenate([jnp.maximum(l[:, j * LANES:(j + 1) * LANES], 0.0) * wt for j in range(nblk)],
                               axis=1)
        acc_s[hf] = acc_s[hf] + term
        return u

    u = lax.fori_loop(0, n_slices, step, jnp.zeros((tq, 1), jnp.int32))
    for hf in range(2):
        score_s[nslot, hf] = jnp.where(vis_half(hf), acc_s[hf], -jnp.inf)
    thr = jnp.where(n_vis <= topk, -jnp.inf, decode(u))

    n_ge = count(lambda s: s >= thr)
    for hf in range(2):
        keep = (score_s[slot, hf] >= thr) & vis_half(hf)
        bias_s[:, hf * half:(hf + 1) * half] = jnp.where(keep, 0.0, NEG_BIG)

    has_tie = jnp.max(jnp.where((n_ge > topk) & (n_vis > topk), 1.0, 0.0)) > 0.0

    @pl.when(has_tie)
    def _():
        need = topk - count(lambda s: s > thr)
        earlier = (lax.broadcasted_iota(jnp.int32, (LANES, LANES), 0)
                   < lax.broadcasted_iota(jnp.int32, (LANES, LANES), 1)).astype(_BF16)
        run = jnp.zeros((tq, 1), _F32)
        for hf in range(2):
            for j in range(nblk):
                c0 = hf * half + j * LANES
                sc = score_s[slot, hf, :, j * LANES:(j + 1) * LANES]
                vis = _visible(tq, LANES, c0, q_pos0, n_valid)
                eq = (sc == thr) & vis
                e = eq.astype(_F32)
                rank = _dot(e.astype(_BF16), earlier) + run
                pick = ((sc > thr) & vis) | (eq & (rank < need))
                bias_s[:, c0:c0 + LANES] = jnp.where(pick, 0.0, NEG_BIG)
                run = run + jnp.sum(e, axis=-1, keepdims=True)

    _softmax_attention(q_ref, gate_ref, kb_ref, vb_ref, bias_s, o_ref, tq=tq, sub=sub, s_len=s_len)


def _attention_tile(obuf, q, qi, wi, gates, kb, vb, kib, *, tq, qblk, s_len, q_pos0, n_valid, topk):
    b, t, _ = q.shape
    sub = min(tq, ATTN_SUB)
    params = pltpu.CompilerParams(dimension_semantics=("arbitrary",), vmem_limit_bytes=VMEM_LIMIT)
    cur = lambda w: pl.BlockSpec((None, tq, w), lambda i: (i, qblk, 0))
    kcur = lambda w: pl.BlockSpec((None, s_len, w), lambda i: (i, 0, 0))
    out_shape = jax.ShapeDtypeStruct((b, t, ATT_WIDTH), _BF16)
    if s_len <= topk:
        assert obuf is None
        kern = functools.partial(_attn_dense_kernel, tq=tq, sub=sub, s_len=s_len, q_pos0=q_pos0, n_valid=n_valid)
        return pl.pallas_call(
            kern, grid=(b,), in_specs=[cur(ATT_WIDTH), cur(ATT_WIDTH), kcur(KV_WIDTH), kcur(KV_WIDTH)],
            out_specs=cur(ATT_WIDTH), out_shape=out_shape,
            scratch_shapes=[pltpu.VMEM((tq, s_len), _F32)], compiler_params=params, name="attention_dense",
        )(q, gates, kb, vb)
    if obuf is None:
        obuf = jnp.zeros(out_shape.shape, _BF16)
    nb = lambda i: jnp.minimum(i + 1, b - 1)
    nxt = lambda w: pl.BlockSpec((None, tq, w), lambda i: (nb(i), qblk, 0))
    knxt = lambda w: pl.BlockSpec((None, s_len, w), lambda i: (nb(i), 0, 0))
    kern = functools.partial(_attn_kernel, tq=tq, sub=sub, s_len=s_len, q_pos0=q_pos0, n_valid=n_valid, topk=topk)
    half = s_len // 2
    return pl.pallas_call(
        kern, grid=(b,),
        in_specs=[cur(ATT_WIDTH), cur(ATT_WIDTH), kcur(KV_WIDTH), kcur(KV_WIDTH),
                  cur(IDX_WIDTH), cur(N_IDX_HEADS), kcur(IDX_DIM),
                  nxt(IDX_WIDTH), nxt(N_IDX_HEADS), knxt(IDX_DIM),
                  pl.BlockSpec(memory_space=pl.ANY)],
        out_specs=cur(ATT_WIDTH), out_shape=out_shape, input_output_aliases={10: 0},
        scratch_shapes=[pltpu.VMEM((2, 2, tq, half), _F32), pltpu.VMEM((2, tq, half), _F32),
                        pltpu.VMEM((tq, s_len), _F32),
                        pltpu.VMEM((N_IDX_HEADS, tq, IDX_DIM), _BF16),
                        pltpu.VMEM((N_IDX_HEADS, tq, LANES), _F32)],
        compiler_params=params, name="attention",
    )(q, gates, kb, vb, qi, wi, kib, qi, wi, kib, obuf)


def _attention(q, qi, wi, gates, kb, vb, kib, *, q_pos0, n_valid, topk, causal_tiles):
    b, t, _ = q.shape
    if causal_tiles:
        tq = _row_tile(t, ATTN_TQ)
        assert q_pos0 == 0 and tq % (2 * LANES) == 0
        obuf = None
        for j in range(t // tq):
            obuf = _attention_tile(obuf, q, qi, wi, gates, kb, vb, kib, tq=tq, qblk=j, s_len=(j + 1) * tq,
                                   q_pos0=j * tq, n_valid=n_valid, topk=topk)
        return obuf
    return _attention_tile(None, q, qi, wi, gates, kb, vb, kib, tq=t, qblk=0, s_len=kb.shape[1],
                           q_pos0=q_pos0, n_valid=n_valid, topk=topk)


def _mix_kernel(u_ref, uprev_ref, ust_ref, pin_ref, pprev_ref, pst_ref, gconv_ref, gpool_ref,
                cw_ref, cb_ref, lng_ref, lnb_ref, pw_ref, poolw_ref, pscale_ref,
                oc_ref, op_ref, xp_s, pp_s, sh_s, *, tt, q_pos0):
    t = pl.program_id(1)

    @pl.when(t == 0)
    def _():
        xp_s[0:CONV_HALO, :] = ust_ref[...]
        pp_s[0:POOL_HALO, :] = pst_ref[...]

    @pl.when(t > 0)
    def _():
        xp_s[0:CONV_HALO, :] = uprev_ref[...]
        pp_s[0:POOL_HALO, :] = pprev_ref[...]

    xp_s[CONV_HALO:CONV_HALO + tt, :] = u_ref[...]
    pin = pin_ref[...]
    pp_s[POOL_HALO:POOL_HALO + tt, :] = pin

    first = CONV_HALO - (CONV_K - 1)
    span = tt + CONV_HALO - SUBLANES
    acc = jnp.zeros((tt, CONV_WIDTH), _F32)
    for r in range(SUBLANES):
        if r:
            sh_s[r - 1] = xp_s[pl.ds(r, span), :]
        for j in range(CONV_K):
            if (first + j) % SUBLANES == r:
                base = first + j - r
                rows = sh_s[r - 1, pl.ds(base, tt), :] if r else xp_s[pl.ds(base, tt), :]
                acc = acc + rows * cw_ref[j:j + 1, :]
    y = acc + cb_ref[...]
    mu = jnp.mean(y, axis=-1, keepdims=True)
    yc = y - mu
    var = jnp.mean(yc * yc, axis=-1, keepdims=True)
    yn = yc * lax.rsqrt(var + EPS) * lng_ref[...] + lnb_ref[...]
    act = (yn * jax.nn.sigmoid(yn)).astype(_BF16)
    oc_ref[...] = (_dot(act, pw_ref[...]) * gconv_ref[...].astype(_F32)).astype(_BF16)

    pos = q_pos0 + t * tt + lax.broadcasted_iota(jnp.int32, (tt, 1), 0)
    for gi, w in enumerate(POOL_WINDOWS):
        cols = slice(gi * POOL_GROUP, (gi + 1) * POOL_GROUP)
        win = pp_s[pl.ds(POOL_HALO, tt), cols]
        for i in range(1, w):
            win = win + pp_s[pl.ds(POOL_HALO - i, tt), cols]
        cnt = jnp.minimum(pos + 1, w).astype(_F32)
        r = (win / cnt - pin[:, cols]).astype(_BF16)
        yp = _dot(r, poolw_ref[gi]) * pscale_ref[:, cols]
        op_ref[:, cols] = (yp * gpool_ref[:, cols].astype(_F32)).astype(_BF16)


def _mixers(u, ust, pin, pst, gates, cw, cb, lng, lnb, pw, poolw, pscale, *, tt, q_pos0):
    b, t, _ = u.shape
    grid = (b, t // tt)
    cur = lambda w: pl.BlockSpec((None, tt, w), lambda i, j: (i, j, 0))

    def prev(rows, src_rows):
        if src_rows < rows:
            return None
        per = tt // rows
        return pl.BlockSpec((None, rows, CONV_WIDTH), lambda i, j: (i, jnp.maximum(j * per - 1, 0), 0))

    uprev_spec = prev(CONV_HALO, t)
    pprev_spec = prev(POOL_HALO, t)
    state = lambda rows: pl.BlockSpec((None, rows, CONV_WIDTH), lambda i, j: (i, 0, 0))
    uprev_src, pprev_src = u, pin
    if uprev_spec is None:
        assert t == tt
        uprev_src, uprev_spec = ust, state(CONV_HALO)
    if pprev_spec is None:
        assert t == tt
        pprev_src, pprev_spec = pst, state(POOL_HALO)
    gate_col = lambda blk: pl.BlockSpec((None, tt, CONV_WIDTH), lambda i, j: (i, j, blk))
    kern = functools.partial(_mix_kernel, tt=tt, q_pos0=q_pos0)
    return pl.pallas_call(
        kern, grid=grid,
        in_specs=[cur(CONV_WIDTH), uprev_spec, state(CONV_HALO), cur(POOL_WIDTH), pprev_spec, state(POOL_HALO),
                  gate_col(ATT_WIDTH // CONV_WIDTH), gate_col(ATT_WIDTH // CONV_WIDTH + 1),
                  _const_spec(cw.shape), _const_spec(cb.shape), _const_spec(lng.shape), _const_spec(lnb.shape),
                  _const_spec(pw.shape), _const_spec(poolw.shape), _const_spec(pscale.shape)],
        out_specs=[cur(CONV_WIDTH), cur(POOL_WIDTH)],
        out_shape=[jax.ShapeDtypeStruct((b, t, CONV_WIDTH), _BF16), jax.ShapeDtypeStruct((b, t, POOL_WIDTH), _BF16)],
        scratch_shapes=[pltpu.VMEM((CONV_HALO + tt, CONV_WIDTH), _F32), pltpu.VMEM((POOL_HALO + tt, POOL_WIDTH), _F32),
                        pltpu.VMEM((SUBLANES - 1, tt + CONV_HALO - SUBLANES, CONV_WIDTH), _F32)],
        compiler_params=pltpu.CompilerParams(dimension_semantics=("arbitrary", "arbitrary"),
                                             vmem_limit_bytes=VMEM_LIMIT),
        name="mixers",
    )(u, uprev_src, ust, pin, pprev_src, pst, gates, gates, cw, cb, lng, lnb, pw, poolw, pscale)


def _outproj_kernel(x_ref, a_ref, c_ref, pl_ref, p_ref, woa_ref, woc_ref, wop_ref, nple_ref,
                    wgate_ref, wproj_ref, nfin_ref, o_ref, *, final):
    x1 = (x_ref[...] + _dot(a_ref[...], woa_ref[...]) + _dot(c_ref[...], woc_ref[...])
          + _dot(pl_ref[...], wop_ref[...]))
    xn = _rms(x1, nple_ref[...]).astype(_BF16)
    gate = jax.nn.sigmoid(_dot(xn, wgate_ref[...]))
    x2 = x1 + gate * _dot(p_ref[...].astype(_BF16), wproj_ref[...])
    o_ref[...] = _rms(x2, nfin_ref[...]) if final else x2


def _outproj(x2, a, c, po, p_all, layer, w_out, nple, wgate, wproj, nfin, *, tm, final):
    n = x2.shape[0]
    row = lambda w: pl.BlockSpec((tm, w), lambda i: (i, 0))
    wblk = lambda rows, blk: pl.BlockSpec((rows, D_MODEL), lambda i: (blk, 0), pipeline_mode=pl.Buffered(1))
    p_spec = pl.BlockSpec((None, tm, PLE_DIM), lambda i: (layer, i, 0))
    return pl.pallas_call(
        functools.partial(_outproj_kernel, final=final), grid=(n // tm,),
        in_specs=[row(D_MODEL), row(ATT_WIDTH), row(CONV_WIDTH), row(POOL_WIDTH), p_spec,
                  wblk(ATT_WIDTH, 0), wblk(CONV_WIDTH, ATT_WIDTH // CONV_WIDTH),
                  wblk(POOL_WIDTH, ATT_WIDTH // CONV_WIDTH + 1),
                  _const_spec(nple.shape), _const_spec(wgate.shape), _const_spec(wproj.shape),
                  _const_spec(nfin.shape)],
        out_specs=row(D_MODEL),
        out_shape=jax.ShapeDtypeStruct((n, D_MODEL), _F32),
        compiler_params=pltpu.CompilerParams(dimension_semantics=("arbitrary",),
                                             vmem_limit_bytes=VMEM_LIMIT),
        name="outproj",
    )(x2, a, c, po, p_all, w_out, w_out, w_out, nple, wgate, wproj, nfin)


def _rope_tables(pos):
    pos = pos.astype(_F32)[:, None]

    def cs(half):
        freq = ROPE_THETA ** (-jnp.arange(half, dtype=_F32) / half)
        ang = pos * freq[None, :]
        return jnp.cos(ang), jnp.sin(ang)

    c, s = cs(HEAD_DIM // 2)
    c128 = jnp.concatenate([c, c], axis=-1)
    s128 = jnp.concatenate([-s, s], axis=-1)
    c, s = cs(IDX_DIM // 2)
    z = jnp.zeros_like(s)
    c64 = jnp.concatenate([c, c, c, c], axis=-1)
    s64a = jnp.concatenate([-s, z, -s, z], axis=-1)
    s64b = jnp.concatenate([z, s, z, s], axis=-1)
    return c128, s128, c64, s64a, s64b


def _split_w_in(w):
    w = w.astype(_BF16)
    cols = lambda a, n: w[:, a:a + n]
    pad = jnp.zeros((w.shape[0], LANES - IDX_DIM - N_IDX_HEADS), _BF16)
    wqk = jnp.concatenate([cols(_OFF_Q, ATT_WIDTH), cols(_OFF_K, KV_WIDTH)], axis=1)
    wv = cols(_OFF_V, KV_WIDTH)
    widx = jnp.concatenate([cols(_OFF_QI, IDX_WIDTH), cols(_OFF_KI, IDX_DIM), cols(_OFF_WI, N_IDX_HEADS), pad], axis=1)
    wgate = jnp.concatenate([cols(_OFF_GATT, ATT_WIDTH), cols(_OFF_GCONV, CONV_WIDTH), cols(_OFF_GPOOL, POOL_WIDTH)], axis=1)
    wglu = cols(_OFF_GLU, 2 * CONV_WIDTH)
    wpool = cols(_OFF_POOL, POOL_WIDTH)
    return wqk, wv, widx, wgate, wglu, wpool


def _row_tile(n, pref):
    t = min(pref, n)
    while n % t:
        t //= 2
    return t


def _layer(x, p_all, layer, pos0, k_past, v_past, ki_past, conv_prev, pool_prev, topk, lw, nfin, final):
    b, t, _ = x.shape
    n = b * t
    (norm_mix, ws, cw, cb, lng, lnb, pw, poolw, pscale, w_out, nple, wgate, wproj) = lw

    tm = _row_tile(t, 256) if b > 1 and t >= 256 else n
    if tm <= t:
        tabs = _rope_tables(pos0 + jnp.arange(t))
        tiles_per_seq = t // tm
    else:
        tabs = tuple(jnp.tile(a, (b, 1)) for a in _rope_tables(pos0 + jnp.arange(t)))
        tiles_per_seq = 1
    (q, k, v, kb, vb, qi, ki, kib, wi, gates, u, pin) = _inproj(
        x.reshape(n, D_MODEL), norm_mix, tabs, ws, tm=tm, tiles_per_seq=tiles_per_seq)
    r3 = lambda a: a.reshape(b, t, a.shape[-1])

    kb3, vb3, kib3 = r3(kb), r3(vb), r3(kib)
    if k_past is not None:
        past = k_past.shape[1]
        kb3 = jnp.concatenate([k_past.reshape(b, past, KV_WIDTH).astype(_BF16), kb3], axis=1)
        vb3 = jnp.concatenate([v_past.reshape(b, past, KV_WIDTH).astype(_BF16), vb3], axis=1)
        kib3 = jnp.concatenate([ki_past.astype(_BF16), kib3], axis=1)
    n_valid = kb3.shape[1]
    s_pad = -(-n_valid // (2 * LANES)) * (2 * LANES)
    if s_pad != n_valid:
        padk = lambda a: jnp.pad(a, ((0, 0), (0, s_pad - n_valid), (0, 0)))
        kb3, vb3, kib3 = padk(kb3), padk(vb3), padk(kib3)
    o_att = _attention(r3(q), r3(qi), r3(wi), r3(gates), kb3, vb3, kib3, q_pos0=pos0, n_valid=n_valid,
                       topk=topk, causal_tiles=k_past is None and t % (2 * LANES) == 0)

    u3, pin3 = r3(u), r3(pin)
    ust = jnp.pad(conv_prev, ((0, 0), (CONV_HALO - (CONV_K - 1), 0), (0, 0)))
    pst = jnp.pad(pool_prev, ((0, 0), (POOL_HALO - POOL_HIST, 0), (0, 0)))
    tt = _row_tile(t, 256)
    oc, op = _mixers(u3, ust, pin3, pst, r3(gates), cw, cb, lng, lnb, pw, poolw, pscale, tt=tt, q_pos0=pos0)

    tm2 = _row_tile(n, 256)
    x_new = _outproj(x.reshape(n, D_MODEL), o_att.reshape(n, ATT_WIDTH), oc.reshape(n, CONV_WIDTH),
                     op.reshape(n, POOL_WIDTH), p_all.reshape(-1, n, PLE_DIM), layer, w_out, nple, wgate, wproj,
                     nfin, tm=tm2, final=final).reshape(b, t, D_MODEL)

    conv_state = jnp.concatenate([conv_prev, u3], axis=1)[:, -(CONV_K - 1):]
    pool_state = jnp.concatenate([pool_prev, pin3], axis=1)[:, -POOL_HIST:]
    return (x_new, k.reshape(b, t, N_KV_HEADS, HEAD_DIM), v.reshape(b, t, N_KV_HEADS, HEAD_DIM),
            r3(ki), conv_state, pool_state)


def kernel(x_prompt, x_sample, p_prompt, p_sample, cache_k, cache_v, cache_kidx, state_conv, state_pool, norm_mix, w_in, conv_w, conv_b, conv_ln_g, conv_ln_b, conv_pw, pool_w, pool_scale, w_out, norm_ple, w_ple_gate, w_ple_proj, norm_final):
    bp, tp, _ = x_prompt.shape
    bs, ts, _ = x_sample.shape
    depth = w_in.shape[0]
    past = cache_k.shape[2]
    topk_p = min(TOPK_MAX, tp // 4)
    topk_s = min(TOPK_MAX, (past + ts) // 4)
    conv0 = jnp.zeros((bp, CONV_K - 1, CONV_WIDTH), x_prompt.dtype)
    pool0 = jnp.zeros((bp, POOL_HIST, POOL_WIDTH), x_prompt.dtype)
    nfin = norm_final.reshape(1, D_MODEL)
    hp, hs = x_prompt, x_sample
    outs_p, outs_s = [], []
    for i in range(depth):
        vec = lambda a: a[i].reshape(1, -1)
        lw = (vec(norm_mix), _split_w_in(w_in[i]), conv_w[i], vec(conv_b), vec(conv_ln_g), vec(conv_ln_b),
              conv_pw[i].astype(_BF16), pool_w[i].astype(_BF16), vec(pool_scale), w_out[i].astype(_BF16),
              vec(norm_ple), w_ple_gate[i].astype(_BF16), w_ple_proj[i].astype(_BF16))
        final = i == depth - 1
        hp, *op_ = _layer(hp, p_prompt, i, 0, None, None, None, conv0, pool0, topk_p, lw, nfin, final)
        hs, *os_ = _layer(hs, p_sample, i, past, cache_k[i], cache_v[i], cache_kidx[i],
                          state_conv[i], state_pool[i], topk_s, lw, nfin, final)
        outs_p.append(op_)
        outs_s.append(os_)
    stack = lambda outs, j: jnp.stack([o[j] for o in outs])
    return ((hp, hs) + tuple(stack(outs_p, j) for j in range(5)) + tuple(stack(outs_s, j) for j in range(5)))
```

```python
import functools

import numpy as np
import jax
import jax.numpy as jnp
from jax import lax
from jax.experimental import pallas as pl
from jax.experimental.pallas import tpu as pltpu

D_MODEL = 2048
CHUNK = 64
HEAD_DIM = 128
ATT_WIDTH = D_MODEL // 2
CONV_WIDTH = D_MODEL // 4
POOL_WIDTH = D_MODEL // 4
N_HEADS = ATT_WIDTH // HEAD_DIM
N_KV_HEADS = 2
HEADS_PER_KV = N_HEADS // N_KV_HEADS
KV_WIDTH = N_KV_HEADS * HEAD_DIM
N_IDX_HEADS = 16
IDX_DIM = 64
IDX_WIDTH = N_IDX_HEADS * IDX_DIM
TOPK_MAX = 256
ROPE_THETA = 10000.0
CONV_K = 31
CONV_HALO = 32
POOL_WINDOWS = (2, 4, 8, 16)
POOL_GROUP = POOL_WIDTH // len(POOL_WINDOWS)
POOL_HIST = 15
POOL_HALO = 16
PLE_DIM = 256
EPS = 1e-6
LANES = 128
SUBLANES = 8
NEG_BIG = -1e30
VMEM_LIMIT = 60 * 1024 * 1024
N_INPROJ_OUTS = 12
ATTN_TQ = 256
ATTN_SUB = 128
ATTN_STEP_ELEMS = 2 * 256 * 768
ATTN_STEP_ROWS = 512
ATTN_MAX_CHAINS = 8

_OFF_Q = 0
_OFF_K = _OFF_Q + ATT_WIDTH
_OFF_V = _OFF_K + KV_WIDTH
_OFF_QI = _OFF_V + KV_WIDTH
_OFF_KI = _OFF_QI + IDX_WIDTH
_OFF_WI = _OFF_KI + IDX_DIM
_OFF_GATT = _OFF_WI + N_IDX_HEADS
_OFF_GLU = _OFF_GATT + ATT_WIDTH
_OFF_GCONV = _OFF_GLU + 2 * CONV_WIDTH
_OFF_POOL = _OFF_GCONV + CONV_WIDTH
_OFF_GPOOL = _OFF_POOL + POOL_WIDTH
N_IN = _OFF_GPOOL + POOL_WIDTH

_F32 = jnp.float32
_BF16 = jnp.bfloat16


def _const_spec(shape):
    nd = len(shape)
    return pl.BlockSpec(shape, lambda *_: (0,) * nd, pipeline_mode=pl.Buffered(1))


def _dot(a, b):
    return jnp.dot(a, b, preferred_element_type=_F32)


def _dot_nt(a, b):
    return lax.dot_general(a, b, (((1,), (1,)), ((), ())), preferred_element_type=_F32)


def _rms(x, g):
    return x * lax.rsqrt(jnp.mean(x * x, axis=-1, keepdims=True) + EPS) * g


def _inproj_kernel(x_ref, g_ref, c128_ref, s128_ref, c64_ref, s64a_ref, s64b_ref,
                   wqk_ref, wv_ref, widx_ref, wgate_ref, wglu_ref, wpool_ref, *rest):
    (q_ref, k_ref, v_ref, kb_ref, vb_ref, qi_ref, ki_ref, kib_ref, wi_ref,
     gates_ref, u_ref, pin_ref) = rest[-N_INPROJ_OUTS:]
    tm = x_ref.shape[0]
    xn = _rms(x_ref[...], g_ref[...]).astype(_BF16)

    zqk = _dot(xn, wqk_ref[...])
    c128 = c128_ref[...]
    s128 = s128_ref[...]
    for h in range(N_HEADS + N_KV_HEADS):
        xh = zqk[:, h * HEAD_DIM:(h + 1) * HEAD_DIM]
        r = xh * c128 + pltpu.roll(xh, HEAD_DIM // 2, 1) * s128
        if h < N_HEADS:
            q_ref[:, h * HEAD_DIM:(h + 1) * HEAD_DIM] = (r * (HEAD_DIM ** -0.5)).astype(_BF16)
        else:
            j = h - N_HEADS
            k_ref[pl.ds(j, tm, stride=N_KV_HEADS), :] = r
            kb_ref[:, j * HEAD_DIM:(j + 1) * HEAD_DIM] = r.astype(_BF16)

    zv = _dot(xn, wv_ref[...])
    for j in range(N_KV_HEADS):
        v_ref[pl.ds(j, tm, stride=N_KV_HEADS), :] = zv[:, j * HEAD_DIM:(j + 1) * HEAD_DIM]
    vb_ref[...] = zv.astype(_BF16)

    zi = _dot(xn, widx_ref[...])
    c64 = c64_ref[...]
    s64a = s64a_ref[...]
    s64b = s64b_ref[...]
    n_groups = IDX_WIDTH // LANES
    for j in range(n_groups + 1):
        xj = zi[:, j * LANES:(j + 1) * LANES]
        r = (xj * c64 + pltpu.roll(xj, LANES - IDX_DIM // 2, 1) * s64a
             + pltpu.roll(xj, IDX_DIM // 2, 1) * s64b)
        if j < n_groups:
            qi_ref[:, j * LANES:(j + 1) * LANES] = r.astype(_BF16)
        else:
            ki_ref[...] = r[:, :IDX_DIM]
            kib_ref[...] = r[:, :IDX_DIM].astype(_BF16)
            wi_ref[...] = xj[:, IDX_DIM:IDX_DIM + N_IDX_HEADS] * (IDX_DIM ** -0.5 * N_IDX_HEADS ** -0.5)

    zg = _dot(xn, wgate_ref[...])
    gates_ref[...] = (zg * jax.nn.sigmoid(zg)).astype(_BF16)

    zu = _dot(xn, wglu_ref[...])
    u_ref[...] = zu[:, :CONV_WIDTH] * jax.nn.sigmoid(zu[:, CONV_WIDTH:])

    pin_ref[...] = _dot(xn, wpool_ref[...])


def _inproj(x2, norm, tabs, ws, kv_prev, *, layer, depth, tm, tiles_per_seq):
    n = x2.shape[0]
    grid = (n // tm,)
    row = lambda w: pl.BlockSpec((tm, w), lambda i: (i, 0))
    tab = pl.BlockSpec((tm, LANES), lambda i: (i % tiles_per_seq, 0))
    kv_shape = jax.ShapeDtypeStruct((depth, n * N_KV_HEADS, HEAD_DIM), _F32)
    kv_spec = pl.BlockSpec((None, tm * N_KV_HEADS, HEAD_DIM), lambda i: (layer, i, 0))
    out_shape = [
        jax.ShapeDtypeStruct((n, ATT_WIDTH), _BF16),
        kv_shape,
        kv_shape,
        jax.ShapeDtypeStruct((n, KV_WIDTH), _BF16),
        jax.ShapeDtypeStruct((n, KV_WIDTH), _BF16),
        jax.ShapeDtypeStruct((n, IDX_WIDTH), _BF16),
        jax.ShapeDtypeStruct((n, IDX_DIM), _F32),
        jax.ShapeDtypeStruct((n, IDX_DIM), _BF16),
        jax.ShapeDtypeStruct((n, N_IDX_HEADS), _F32),
        jax.ShapeDtypeStruct((n, D_MODEL), _BF16),
        jax.ShapeDtypeStruct((n, CONV_WIDTH), _F32),
        jax.ShapeDtypeStruct((n, POOL_WIDTH), _F32),
    ]
    assert len(out_shape) == N_INPROJ_OUTS
    out_specs = [kv_spec if s is kv_shape else row(s.shape[1]) for s in out_shape]
    in_specs = ([row(D_MODEL), _const_spec((1, D_MODEL))] + [tab] * 5
                + [_const_spec(w.shape) for w in ws])
    args = (x2, norm, *tabs, *ws)
    aliases = {}
    if kv_prev is not None:
        aliases = {len(args): 1, len(args) + 1: 2}
        in_specs = in_specs + [pl.BlockSpec(memory_space=pl.ANY)] * 2
        args = args + tuple(kv_prev)
    return pl.pallas_call(
        _inproj_kernel, grid=grid, in_specs=in_specs, out_specs=out_specs, out_shape=out_shape,
        input_output_aliases=aliases,
        compiler_params=pltpu.CompilerParams(dimension_semantics=("arbitrary",),
                                             vmem_limit_bytes=VMEM_LIMIT),
        name="inproj",
    )(*args)


def _softmax_attention(q_ref, gate_ref, kb_ref, vb_ref, bias_s, o_ref, *, tq, sub, s_len):
    for r0 in range(0, tq, sub):
        rows = slice(r0, r0 + sub)
        q = q_ref[rows, :]
        gate = gate_ref[rows, :].astype(_F32)
        bias = bias_s[rows, :]
        for g in range(N_KV_HEADS):
            heads = [g * HEADS_PER_KV + r for r in range(HEADS_PER_KV)]
            qg = jnp.concatenate([q[:, h * HEAD_DIM:(h + 1) * HEAD_DIM] for h in heads], axis=0)
            s = _dot_nt(qg, kb_ref[:, g * HEAD_DIM:(g + 1) * HEAD_DIM])
            s = (s.reshape(HEADS_PER_KV, sub, s_len) + bias[None]).reshape(HEADS_PER_KV * sub, s_len)
            m = jnp.max(s, axis=-1, keepdims=True)
            p = jnp.exp(s - m)
            den = jnp.sum(p, axis=-1, keepdims=True)
            o = _dot(p.astype(_BF16), vb_ref[:, g * HEAD_DIM:(g + 1) * HEAD_DIM]) / den
            for r, h in enumerate(heads):
                cols = slice(h * HEAD_DIM, (h + 1) * HEAD_DIM)
                o_ref[rows, cols] = (o[r * sub:(r + 1) * sub] * gate[:, cols]).astype(_BF16)


def _visible(tq, cols, col0, q_pos0, n_valid):
    qpos = q_pos0 + lax.broadcasted_iota(jnp.int32, (tq, 1), 0)
    kpos = col0 + lax.broadcasted_iota(jnp.int32, (1, cols), 1)
    return (lax.div(kpos, jnp.int32(CHUNK)) <= lax.div(qpos, jnp.int32(CHUNK))) & (kpos < n_valid)


def _attn_dense_kernel(q_ref, gate_ref, kb_ref, vb_ref, o_ref, bias_s, *, tq, sub, s_len, q_pos0, n_valid):
    bias_s[...] = jnp.where(_visible(tq, s_len, 0, q_pos0, n_valid), 0.0, NEG_BIG)
    _softmax_attention(q_ref, gate_ref, kb_ref, vb_ref, bias_s, o_ref, tq=tq, sub=sub, s_len=s_len)


def _attn_kernel(q_ref, gate_ref, kb_ref, vb_ref, qi0_ref, wi0_ref, kib0_ref, qi1_ref, wi1_ref, kib1_ref,
                 obuf_ref, o_ref, score_s, acc_s, bias_s, qi_s, w_s, *, nb, tq, sub, s_len, q_pos0, n_valid, topk):
    del obuf_ref
    b = pl.program_id(0)
    slot = lax.rem(b, 2)
    nslot = 1 - slot
    half = s_len // 2
    nblk = half // LANES
    n_slices = 2 * N_IDX_HEADS
    assert n_slices == 32
    tiles = range(nb)

    qpos = q_pos0 + lax.broadcasted_iota(jnp.int32, (tq, 1), 0)
    n_vis = jnp.minimum((lax.div(qpos, jnp.int32(CHUNK)) + 1) * CHUNK, n_valid)
    vis_half = lambda hf: _visible(tq, half, hf * half, q_pos0, n_valid)

    @pl.when(b == 0)
    def _():
        for bb in tiles:
            qi = qi0_ref[bb]
            wi = wi0_ref[bb]
            for hf in range(2):
                kib = kib0_ref[bb, hf * half:(hf + 1) * half, :]
                acc = jnp.zeros((tq, half), _F32)
                for h in range(N_IDX_HEADS):
                    l = _dot_nt(qi[:, h * IDX_DIM:(h + 1) * IDX_DIM], kib)
                    acc = acc + jnp.maximum(l, 0.0) * wi[:, h:h + 1]
                score_s[bb, 0, hf] = jnp.where(vis_half(hf), acc, -jnp.inf)

    for bb in tiles:
        qi1 = qi1_ref[bb]
        wi1 = wi1_ref[bb]
        for h in range(N_IDX_HEADS):
            qi_s[bb, h] = qi1[:, h * IDX_DIM:(h + 1) * IDX_DIM]
            w_s[bb, h] = jnp.broadcast_to(wi1[:, h:h + 1], (tq, LANES))
    acc_s[...] = jnp.zeros((nb, 2, tq, half), _F32)

    int_min = jnp.int32(-2 ** 31)

    def decode(u):
        sk = u ^ int_min
        bits = jnp.where(sk < 0, sk ^ jnp.int32(0x7FFFFFFF), sk)
        return lax.bitcast_convert_type(bits, _F32)

    def count(bb, pred):
        return (jnp.sum(pred(score_s[bb, slot, 0]).astype(_F32), axis=-1, keepdims=True)
                + jnp.sum(pred(score_s[bb, slot, 1]).astype(_F32), axis=-1, keepdims=True))

    def step(i, us):
        h = lax.div(i, jnp.int32(2))
        hf = lax.rem(i, jnp.int32(2))
        out = []
        for bb, u in zip(tiles, us):
            cand = u | lax.shift_left(jnp.int32(1), 31 - i)
            t = decode(cand)
            out.append(jnp.where(count(bb, lambda s: s >= t) >= topk, cand, u))
            l = _dot_nt(qi_s[bb, h], kib1_ref[bb, pl.ds(pl.multiple_of(hf * half, LANES), half), :])
            wt = w_s[bb, h]
            term = jnp.concatenate([jnp.maximum(l[:, j * LANES:(j + 1) * LANES], 0.0) * wt for j in range(nblk)],
                                   axis=1)
            acc_s[bb, hf] = acc_s[bb, hf] + term
        return tuple(out)

    us = lax.fori_loop(0, n_slices, step, tuple(jnp.zeros((tq, 1), jnp.int32) for _ in tiles))
    thrs, flags = [], []
    for bb, u in zip(tiles, us):
        for hf in range(2):
            score_s[bb, nslot, hf] = jnp.where(vis_half(hf), acc_s[bb, hf], -jnp.inf)
        thr = jnp.where(n_vis <= topk, -jnp.inf, decode(u))
        n_ge = count(bb, lambda s: s >= thr)
        for hf in range(2):
            keep = (score_s[bb, slot, hf] >= thr) & vis_half(hf)
            bias_s[bb, :, hf * half:(hf + 1) * half] = jnp.where(keep, 0.0, NEG_BIG)
        thrs.append(thr)
        flags.append(jnp.max(jnp.where((n_ge > topk) & (n_vis > topk), 1.0, 0.0)))

    @pl.when(functools.reduce(jnp.maximum, flags) > 0.0)
    def _():
        earlier = (lax.broadcasted_iota(jnp.int32, (LANES, LANES), 0)
                   < lax.broadcasted_iota(jnp.int32, (LANES, LANES), 1)).astype(_BF16)
        for bb, thr in zip(tiles, thrs):
            need = topk - count(bb, lambda s: s > thr)
            run = jnp.zeros((tq, 1), _F32)
            for hf in range(2):
                for j in range(nblk):
                    c0 = hf * half + j * LANES
                    sc = score_s[bb, slot, hf, :, j * LANES:(j + 1) * LANES]
                    vis = _visible(tq, LANES, c0, q_pos0, n_valid)
                    eq = (sc == thr) & vis
                    e = eq.astype(_F32)
                    rank = _dot(e.astype(_BF16), earlier) + run
                    pick = ((sc > thr) & vis) | (eq & (rank < need))
                    bias_s[bb, :, c0:c0 + LANES] = jnp.where(pick, 0.0, NEG_BIG)
                    run = run + jnp.sum(e, axis=-1, keepdims=True)

    for bb in tiles:
        _softmax_attention(q_ref.at[bb], gate_ref.at[bb], kb_ref.at[bb], vb_ref.at[bb], bias_s.at[bb], o_ref.at[bb],
                           tq=tq, sub=sub, s_len=s_len)


def _attention_tile(obuf, q, qi, wi, gates, kb, vb, kib, *, tq, qblk, s_len, q_pos0, n_valid, topk):
    b, t, _ = q.shape
    sub = min(tq, ATTN_SUB)
    params = pltpu.CompilerParams(dimension_semantics=("arbitrary",), vmem_limit_bytes=VMEM_LIMIT)
    cur = lambda w: pl.BlockSpec((None, tq, w), lambda i: (i, qblk, 0))
    kcur = lambda w: pl.BlockSpec((None, s_len, w), lambda i: (i, 0, 0))
    out_shape = jax.ShapeDtypeStruct((b, t, ATT_WIDTH), _BF16)
    if s_len <= topk:
        assert obuf is None
        kern = functools.partial(_attn_dense_kernel, tq=tq, sub=sub, s_len=s_len, q_pos0=q_pos0, n_valid=n_valid)
        return pl.pallas_call(
            kern, grid=(b,), in_specs=[cur(ATT_WIDTH), cur(ATT_WIDTH), kcur(KV_WIDTH), kcur(KV_WIDTH)],
            out_specs=cur(ATT_WIDTH), out_shape=out_shape,
            scratch_shapes=[pltpu.VMEM((tq, s_len), _F32)], compiler_params=params, name="attention_dense",
        )(q, gates, kb, vb)
    if obuf is None:
        obuf = jnp.zeros(out_shape.shape, _BF16)
    nb = 1
    while (nb < ATTN_MAX_CHAINS and b % (2 * nb) == 0 and 2 * nb * tq <= ATTN_STEP_ROWS
           and 2 * nb * tq * s_len <= ATTN_STEP_ELEMS):
        nb *= 2
    steps = b // nb
    ahead = lambda i: jnp.minimum(i + 1, steps - 1)
    cur = lambda w: pl.BlockSpec((nb, tq, w), lambda i: (i, qblk, 0))
    kcur = lambda w: pl.BlockSpec((nb, s_len, w), lambda i: (i, 0, 0))
    nxt = lambda w: pl.BlockSpec((nb, tq, w), lambda i: (ahead(i), qblk, 0))
    knxt = lambda w: pl.BlockSpec((nb, s_len, w), lambda i: (ahead(i), 0, 0))
    kern = functools.partial(_attn_kernel, nb=nb, tq=tq, sub=sub, s_len=s_len, q_pos0=q_pos0, n_valid=n_valid,
                             topk=topk)
    half = s_len // 2
    return pl.pallas_call(
        kern, grid=(steps,),
        in_specs=[cur(ATT_WIDTH), cur(ATT_WIDTH), kcur(KV_WIDTH), kcur(KV_WIDTH),
                  cur(IDX_WIDTH), cur(N_IDX_HEADS), kcur(IDX_DIM),
                  nxt(IDX_WIDTH), nxt(N_IDX_HEADS), knxt(IDX_DIM),
                  pl.BlockSpec(memory_space=pl.ANY)],
        out_specs=cur(ATT_WIDTH), out_shape=out_shape, input_output_aliases={10: 0},
        scratch_shapes=[pltpu.VMEM((nb, 2, 2, tq, half), _F32), pltpu.VMEM((nb, 2, tq, half), _F32),
                        pltpu.VMEM((nb, tq, s_len), _F32),
                        pltpu.VMEM((nb, N_IDX_HEADS, tq, IDX_DIM), _BF16),
                        pltpu.VMEM((nb, N_IDX_HEADS, tq, LANES), _F32)],
        compiler_params=params, name="attention",
    )(q, gates, kb, vb, qi, wi, kib, qi, wi, kib, obuf)


def _attention(q, qi, wi, gates, kb, vb, kib, *, q_pos0, n_valid, topk, causal_tiles):
    b, t, _ = q.shape
    if causal_tiles:
        tq = _row_tile(t, ATTN_TQ)
        assert q_pos0 == 0 and tq % (2 * LANES) == 0
        obuf = None
        for j in range(t // tq):
            obuf = _attention_tile(obuf, q, qi, wi, gates, kb, vb, kib, tq=tq, qblk=j, s_len=(j + 1) * tq,
                                   q_pos0=j * tq, n_valid=n_valid, topk=topk)
        return obuf
    return _attention_tile(None, q, qi, wi, gates, kb, vb, kib, tq=t, qblk=0, s_len=kb.shape[1],
                           q_pos0=q_pos0, n_valid=n_valid, topk=topk)


def _mix_kernel(u_ref, uprev_ref, ust_ref, pin_ref, pprev_ref, pst_ref, gconv_ref, gpool_ref,
                cw_ref, cb_ref, lng_ref, lnb_ref, pw_ref, poolw_ref, pscale_ref,
                oc_ref, op_ref, xp_s, pp_s, sh_s, *, tt, q_pos0):
    t = pl.program_id(1)

    @pl.when(t == 0)
    def _():
        xp_s[0:CONV_HALO, :] = ust_ref[...]
        pp_s[0:POOL_HALO, :] = pst_ref[...]

    @pl.when(t > 0)
    def _():
        xp_s[0:CONV_HALO, :] = uprev_ref[...]
        pp_s[0:POOL_HALO, :] = pprev_ref[...]

    xp_s[CONV_HALO:CONV_HALO + tt, :] = u_ref[...]
    pin = pin_ref[...]
    pp_s[POOL_HALO:POOL_HALO + tt, :] = pin

    first = CONV_HALO - (CONV_K - 1)
    span = tt + CONV_HALO - SUBLANES
    acc = jnp.zeros((tt, CONV_WIDTH), _F32)
    for r in range(SUBLANES):
        if r:
            sh_s[r - 1] = xp_s[pl.ds(r, span), :]
        for j in range(CONV_K):
            if (first + j) % SUBLANES == r:
                base = first + j - r
                rows = sh_s[r - 1, pl.ds(base, tt), :] if r else xp_s[pl.ds(base, tt), :]
                acc = acc + rows * cw_ref[j:j + 1, :]
    y = acc + cb_ref[...]
    mu = jnp.mean(y, axis=-1, keepdims=True)
    yc = y - mu
    var = jnp.mean(yc * yc, axis=-1, keepdims=True)
    yn = yc * lax.rsqrt(var + EPS) * lng_ref[...] + lnb_ref[...]
    act = (yn * jax.nn.sigmoid(yn)).astype(_BF16)
    oc_ref[...] = (_dot(act, pw_ref[...]) * gconv_ref[...].astype(_F32)).astype(_BF16)

    pos = q_pos0 + t * tt + lax.broadcasted_iota(jnp.int32, (tt, 1), 0)
    for gi, w in enumerate(POOL_WINDOWS):
        cols = slice(gi * POOL_GROUP, (gi + 1) * POOL_GROUP)
        win = pp_s[pl.ds(POOL_HALO, tt), cols]
        for i in range(1, w):
            win = win + pp_s[pl.ds(POOL_HALO - i, tt), cols]
        cnt = jnp.minimum(pos + 1, w).astype(_F32)
        r = (win / cnt - pin[:, cols]).astype(_BF16)
        yp = _dot(r, poolw_ref[gi]) * pscale_ref[:, cols]
        op_ref[:, cols] = (yp * gpool_ref[:, cols].astype(_F32)).astype(_BF16)


def _mixers(u, ust, pin, pst, gates, cw, cb, lng, lnb, pw, poolw, pscale, *, tt, q_pos0):
    b, t, _ = u.shape
    grid = (b, t // tt)
    cur = lambda w: pl.BlockSpec((None, tt, w), lambda i, j: (i, j, 0))

    def prev(rows, src_rows):
        if src_rows < rows:
            return None
        per = tt // rows
        return pl.BlockSpec((None, rows, CONV_WIDTH), lambda i, j: (i, jnp.maximum(j * per - 1, 0), 0))

    uprev_spec = prev(CONV_HALO, t)
    pprev_spec = prev(POOL_HALO, t)
    state = lambda rows: pl.BlockSpec((None, rows, CONV_WIDTH), lambda i, j: (i, 0, 0))
    uprev_src, pprev_src = u, pin
    if uprev_spec is None:
        assert t == tt
        uprev_src, uprev_spec = ust, state(CONV_HALO)
    if pprev_spec is None:
        assert t == tt
        pprev_src, pprev_spec = pst, state(POOL_HALO)
    gate_col = lambda blk: pl.BlockSpec((None, tt, CONV_WIDTH), lambda i, j: (i, j, blk))
    kern = functools.partial(_mix_kernel, tt=tt, q_pos0=q_pos0)
    return pl.pallas_call(
        kern, grid=grid,
        in_specs=[cur(CONV_WIDTH), uprev_spec, state(CONV_HALO), cur(POOL_WIDTH), pprev_spec, state(POOL_HALO),
                  gate_col(ATT_WIDTH // CONV_WIDTH), gate_col(ATT_WIDTH // CONV_WIDTH + 1),
                  _const_spec(cw.shape), _const_spec(cb.shape), _const_spec(lng.shape), _const_spec(lnb.shape),
                  _const_spec(pw.shape), _const_spec(poolw.shape), _const_spec(pscale.shape)],
        out_specs=[cur(CONV_WIDTH), cur(POOL_WIDTH)],
        out_shape=[jax.ShapeDtypeStruct((b, t, CONV_WIDTH), _BF16), jax.ShapeDtypeStruct((b, t, POOL_WIDTH), _BF16)],
        scratch_shapes=[pltpu.VMEM((CONV_HALO + tt, CONV_WIDTH), _F32), pltpu.VMEM((POOL_HALO + tt, POOL_WIDTH), _F32),
                        pltpu.VMEM((SUBLANES - 1, tt + CONV_HALO - SUBLANES, CONV_WIDTH), _F32)],
        compiler_params=pltpu.CompilerParams(dimension_semantics=("arbitrary", "arbitrary"),
                                             vmem_limit_bytes=VMEM_LIMIT),
        name="mixers",
    )(u, uprev_src, ust, pin, pprev_src, pst, gates, gates, cw, cb, lng, lnb, pw, poolw, pscale)


def _outproj_kernel(x_ref, a_ref, c_ref, pl_ref, p_ref, woa_ref, woc_ref, wop_ref, nple_ref,
                    wgate_ref, wproj_ref, nfin_ref, o_ref, *, final):
    x1 = (x_ref[...] + _dot(a_ref[...], woa_ref[...]) + _dot(c_ref[...], woc_ref[...])
          + _dot(pl_ref[...], wop_ref[...]))
    xn = _rms(x1, nple_ref[...]).astype(_BF16)
    gate = jax.nn.sigmoid(_dot(xn, wgate_ref[...]))
    x2 = x1 + gate * _dot(p_ref[...].astype(_BF16), wproj_ref[...])
    o_ref[...] = _rms(x2, nfin_ref[...]) if final else x2


def _outproj(x2, a, c, po, p_all, layer, w_out, nple, wgate, wproj, nfin, *, tm, final):
    n = x2.shape[0]
    row = lambda w: pl.BlockSpec((tm, w), lambda i: (i, 0))
    wblk = lambda rows, blk: pl.BlockSpec((rows, D_MODEL), lambda i: (blk, 0), pipeline_mode=pl.Buffered(1))
    p_spec = pl.BlockSpec((None, tm, PLE_DIM), lambda i: (layer, i, 0))
    return pl.pallas_call(
        functools.partial(_outproj_kernel, final=final), grid=(n // tm,),
        in_specs=[row(D_MODEL), row(ATT_WIDTH), row(CONV_WIDTH), row(POOL_WIDTH), p_spec,
                  wblk(ATT_WIDTH, 0), wblk(CONV_WIDTH, ATT_WIDTH // CONV_WIDTH),
                  wblk(POOL_WIDTH, ATT_WIDTH // CONV_WIDTH + 1),
                  _const_spec(nple.shape), _const_spec(wgate.shape), _const_spec(wproj.shape),
                  _const_spec(nfin.shape)],
        out_specs=row(D_MODEL),
        out_shape=jax.ShapeDtypeStruct((n, D_MODEL), _F32),
        compiler_params=pltpu.CompilerParams(dimension_semantics=("arbitrary",),
                                             vmem_limit_bytes=VMEM_LIMIT),
        name="outproj",
    )(x2, a, c, po, p_all, w_out, w_out, w_out, nple, wgate, wproj, nfin)


def _rope_tables(pos):
    pos = pos.astype(_F32)[:, None]

    def cs(half):
        freq = ROPE_THETA ** (-jnp.arange(half, dtype=_F32) / half)
        ang = pos * freq[None, :]
        return jnp.cos(ang), jnp.sin(ang)

    c, s = cs(HEAD_DIM // 2)
    c128 = jnp.concatenate([c, c], axis=-1)
    s128 = jnp.concatenate([-s, s], axis=-1)
    c, s = cs(IDX_DIM // 2)
    z = jnp.zeros_like(s)
    c64 = jnp.concatenate([c, c, c, c], axis=-1)
    s64a = jnp.concatenate([-s, z, -s, z], axis=-1)
    s64b = jnp.concatenate([z, s, z, s], axis=-1)
    return c128, s128, c64, s64a, s64b


def _split_w_in(w):
    w = w.astype(_BF16)
    cols = lambda a, n: w[:, a:a + n]
    pad = jnp.zeros((w.shape[0], LANES - IDX_DIM - N_IDX_HEADS), _BF16)
    wqk = jnp.concatenate([cols(_OFF_Q, ATT_WIDTH), cols(_OFF_K, KV_WIDTH)], axis=1)
    wv = cols(_OFF_V, KV_WIDTH)
    widx = jnp.concatenate([cols(_OFF_QI, IDX_WIDTH), cols(_OFF_KI, IDX_DIM), cols(_OFF_WI, N_IDX_HEADS), pad], axis=1)
    wgate = jnp.concatenate([cols(_OFF_GATT, ATT_WIDTH), cols(_OFF_GCONV, CONV_WIDTH), cols(_OFF_GPOOL, POOL_WIDTH)], axis=1)
    wglu = cols(_OFF_GLU, 2 * CONV_WIDTH)
    wpool = cols(_OFF_POOL, POOL_WIDTH)
    return wqk, wv, widx, wgate, wglu, wpool


def _row_tile(n, pref):
    t = min(pref, n)
    while n % t:
        t //= 2
    return t


def _layer(x, p_all, kv_prev, layer, pos0, k_past, v_past, ki_past, conv_prev, pool_prev, topk, lw, nfin, final):
    b, t, _ = x.shape
    n = b * t
    (norm_mix, ws, cw, cb, lng, lnb, pw, poolw, pscale, w_out, nple, wgate, wproj) = lw

    tm = _row_tile(t, 256) if b > 1 and t >= 256 else n
    if tm <= t:
        tabs = _rope_tables(pos0 + jnp.arange(t))
        tiles_per_seq = t // tm
    else:
        tabs = tuple(jnp.tile(a, (b, 1)) for a in _rope_tables(pos0 + jnp.arange(t)))
        tiles_per_seq = 1
    (q, k_all, v_all, kb, vb, qi, ki, kib, wi, gates, u, pin) = _inproj(
        x.reshape(n, D_MODEL), norm_mix, tabs, ws, kv_prev, layer=layer, depth=p_all.shape[0], tm=tm,
        tiles_per_seq=tiles_per_seq)
    r3 = lambda a: a.reshape(b, t, a.shape[-1])

    kb3, vb3, kib3 = r3(kb), r3(vb), r3(kib)
    if k_past is not None:
        past = k_past.shape[1]
        kb3 = jnp.concatenate([k_past.reshape(b, past, KV_WIDTH).astype(_BF16), kb3], axis=1)
        vb3 = jnp.concatenate([v_past.reshape(b, past, KV_WIDTH).astype(_BF16), vb3], axis=1)
        kib3 = jnp.concatenate([ki_past.astype(_BF16), kib3], axis=1)
    n_valid = kb3.shape[1]
    s_pad = -(-n_valid // (2 * LANES)) * (2 * LANES)
    if s_pad != n_valid:
        padk = lambda a: jnp.pad(a, ((0, 0), (0, s_pad - n_valid), (0, 0)))
        kb3, vb3, kib3 = padk(kb3), padk(vb3), padk(kib3)
    o_att = _attention(r3(q), r3(qi), r3(wi), r3(gates), kb3, vb3, kib3, q_pos0=pos0, n_valid=n_valid,
                       topk=topk, causal_tiles=k_past is None and t % (2 * LANES) == 0)

    u3, pin3 = r3(u), r3(pin)
    ust = jnp.pad(conv_prev, ((0, 0), (CONV_HALO - (CONV_K - 1), 0), (0, 0)))
    pst = jnp.pad(pool_prev, ((0, 0), (POOL_HALO - POOL_HIST, 0), (0, 0)))
    tt = _row_tile(t, 256)
    oc, op = _mixers(u3, ust, pin3, pst, r3(gates), cw, cb, lng, lnb, pw, poolw, pscale, tt=tt, q_pos0=pos0)

    tm2 = _row_tile(n, 256)
    x_new = _outproj(x.reshape(n, D_MODEL), o_att.reshape(n, ATT_WIDTH), oc.reshape(n, CONV_WIDTH),
                     op.reshape(n, POOL_WIDTH), p_all.reshape(-1, n, PLE_DIM), layer, w_out, nple, wgate, wproj,
                     nfin, tm=tm2, final=final).reshape(b, t, D_MODEL)

    conv_state = jnp.concatenate([conv_prev, u3], axis=1)[:, -(CONV_K - 1):]
    pool_state = jnp.concatenate([pool_prev, pin3], axis=1)[:, -POOL_HIST:]
    return x_new, (k_all, v_all), r3(ki), conv_state, pool_state


def kernel(x_prompt, x_sample, p_prompt, p_sample, cache_k, cache_v, cache_kidx, state_conv, state_pool, norm_mix, w_in, conv_w, conv_b, conv_ln_g, conv_ln_b, conv_pw, pool_w, pool_scale, w_out, norm_ple, w_ple_gate, w_ple_proj, norm_final):
    bp, tp, _ = x_prompt.shape
    bs, ts, _ = x_sample.shape
    depth = w_in.shape[0]
    past = cache_k.shape[2]
    topk_p = min(TOPK_MAX, tp // 4)
    topk_s = min(TOPK_MAX, (past + ts) // 4)
    conv0 = jnp.zeros((bp, CONV_K - 1, CONV_WIDTH), x_prompt.dtype)
    pool0 = jnp.zeros((bp, POOL_HIST, POOL_WIDTH), x_prompt.dtype)
    nfin = norm_final.reshape(1, D_MODEL)
    hp, hs = x_prompt, x_sample
    kv_p = kv_s = None
    outs_p, outs_s = [], []
    for i in range(depth):
        vec = lambda a: a[i].reshape(1, -1)
        lw = (vec(norm_mix), _split_w_in(w_in[i]), conv_w[i], vec(conv_b), vec(conv_ln_g), vec(conv_ln_b),
              conv_pw[i].astype(_BF16), pool_w[i].astype(_BF16), vec(pool_scale), w_out[i].astype(_BF16),
              vec(norm_ple), w_ple_gate[i].astype(_BF16), w_ple_proj[i].astype(_BF16))
        final = i == depth - 1
        hp, kv_p, *op_ = _layer(hp, p_prompt, kv_p, i, 0, None, None, None, conv0, pool0, topk_p, lw, nfin, final)
        hs, kv_s, *os_ = _layer(hs, p_sample, kv_s, i, past, cache_k[i], cache_v[i], cache_kidx[i],
                                state_conv[i], state_pool[i], topk_s, lw, nfin, final)
        outs_p.append(op_)
        outs_s.append(os_)
    stack = lambda outs, j: jnp.stack([o[j] for o in outs])
    heads = lambda a, b, t: a.reshape(depth, b, t, N_KV_HEADS, HEAD_DIM)
    return ((hp, hs, heads(kv_p[0], bp, tp), heads(kv_p[1], bp, tp)) + tuple(stack(outs_p, j) for j in range(3))
            + (heads(kv_s[0], bs, ts), heads(kv_s[1], bs, ts)) + tuple(stack(outs_s, j) for j in range(3)))
```

```python
import functools

import numpy as np
import jax
import jax.numpy as jnp
from jax import lax
from jax.experimental import pallas as pl
from jax.experimental.pallas import tpu as pltpu

D_MODEL = 2048
CHUNK = 64
HEAD_DIM = 128
ATT_WIDTH = D_MODEL // 2
CONV_WIDTH = D_MODEL // 4
POOL_WIDTH = D_MODEL // 4
N_HEADS = ATT_WIDTH // HEAD_DIM
N_KV_HEADS = 2
HEADS_PER_KV = N_HEADS // N_KV_HEADS
KV_WIDTH = N_KV_HEADS * HEAD_DIM
N_IDX_HEADS = 16
IDX_DIM = 64
IDX_WIDTH = N_IDX_HEADS * IDX_DIM
TOPK_MAX = 256
ROPE_THETA = 10000.0
CONV_K = 31
CONV_HALO = 32
POOL_WINDOWS = (2, 4, 8, 16)
POOL_GROUP = POOL_WIDTH // len(POOL_WINDOWS)
POOL_HIST = 15
POOL_HALO = 16
PLE_DIM = 256
EPS = 1e-6
LANES = 128
SUBLANES = 8
NEG_BIG = -1e30
VMEM_LIMIT = 60 * 1024 * 1024
N_INPROJ_OUTS = 12
ATTN_TQ = 256
ATTN_SUB = 128
ATTN_SUB_LONG = 64
ATTN_LONG_KEYS = 1536
ATTN_STEP_ELEMS = 2 * 256 * 768
ATTN_STEP_ROWS = 512
ATTN_MAX_CHAINS = 8

_OFF_Q = 0
_OFF_K = _OFF_Q + ATT_WIDTH
_OFF_V = _OFF_K + KV_WIDTH
_OFF_QI = _OFF_V + KV_WIDTH
_OFF_KI = _OFF_QI + IDX_WIDTH
_OFF_WI = _OFF_KI + IDX_DIM
_OFF_GATT = _OFF_WI + N_IDX_HEADS
_OFF_GLU = _OFF_GATT + ATT_WIDTH
_OFF_GCONV = _OFF_GLU + 2 * CONV_WIDTH
_OFF_POOL = _OFF_GCONV + CONV_WIDTH
_OFF_GPOOL = _OFF_POOL + POOL_WIDTH
N_IN = _OFF_GPOOL + POOL_WIDTH

_F32 = jnp.float32
_BF16 = jnp.bfloat16


def _const_spec(shape):
    nd = len(shape)
    return pl.BlockSpec(shape, lambda *_: (0,) * nd, pipeline_mode=pl.Buffered(1))


def _dot(a, b):
    return jnp.dot(a, b, preferred_element_type=_F32)


def _dot_nt(a, b):
    return lax.dot_general(a, b, (((1,), (1,)), ((), ())), preferred_element_type=_F32)


def _rms(x, g):
    return x * lax.rsqrt(jnp.mean(x * x, axis=-1, keepdims=True) + EPS) * g


def _inproj_kernel(x_ref, g_ref, c128_ref, s128_ref, c64_ref, s64a_ref, s64b_ref,
                   wqk_ref, wv_ref, widx_ref, wgate_ref, wglu_ref, wpool_ref, *rest, first_plane):
    (q_ref, k_ref, v_ref, kb_ref, vb_ref, qi_ref, ki_ref, kib_ref, wi_ref,
     gates_ref, u_ref, pin_ref) = rest[-N_INPROJ_OUTS:]
    tm = x_ref.shape[0]
    if first_plane is not None:
        for ref in (k_ref, v_ref):
            for d in range(ref.shape[0]):
                if d != first_plane:
                    ref[d] = jnp.zeros(ref.shape[1:], _F32)
        k_ref, v_ref = k_ref.at[first_plane], v_ref.at[first_plane]
    xn = _rms(x_ref[...], g_ref[...]).astype(_BF16)

    zqk = _dot(xn, wqk_ref[...])
    c128 = c128_ref[...]
    s128 = s128_ref[...]
    for h in range(N_HEADS + N_KV_HEADS):
        xh = zqk[:, h * HEAD_DIM:(h + 1) * HEAD_DIM]
        r = xh * c128 + pltpu.roll(xh, HEAD_DIM // 2, 1) * s128
        if h < N_HEADS:
            q_ref[:, h * HEAD_DIM:(h + 1) * HEAD_DIM] = (r * (HEAD_DIM ** -0.5)).astype(_BF16)
        else:
            j = h - N_HEADS
            k_ref[pl.ds(j, tm, stride=N_KV_HEADS), :] = r
            kb_ref[:, j * HEAD_DIM:(j + 1) * HEAD_DIM] = r.astype(_BF16)

    zv = _dot(xn, wv_ref[...])
    for j in range(N_KV_HEADS):
        v_ref[pl.ds(j, tm, stride=N_KV_HEADS), :] = zv[:, j * HEAD_DIM:(j + 1) * HEAD_DIM]
    vb_ref[...] = zv.astype(_BF16)

    zi = _dot(xn, widx_ref[...])
    c64 = c64_ref[...]
    s64a = s64a_ref[...]
    s64b = s64b_ref[...]
    n_groups = IDX_WIDTH // LANES
    for j in range(n_groups + 1):
        xj = zi[:, j * LANES:(j + 1) * LANES]
        r = (xj * c64 + pltpu.roll(xj, LANES - IDX_DIM // 2, 1) * s64a
             + pltpu.roll(xj, IDX_DIM // 2, 1) * s64b)
        if j < n_groups:
            qi_ref[:, j * LANES:(j + 1) * LANES] = r.astype(_BF16)
        else:
            ki_ref[...] = r[:, :IDX_DIM]
            kib_ref[...] = r[:, :IDX_DIM].astype(_BF16)
            wi_ref[...] = xj[:, IDX_DIM:IDX_DIM + N_IDX_HEADS] * (IDX_DIM ** -0.5 * N_IDX_HEADS ** -0.5)

    zg = _dot(xn, wgate_ref[...])
    gates_ref[...] = (zg * jax.nn.sigmoid(zg)).astype(_BF16)

    zu = _dot(xn, wglu_ref[...])
    u_ref[...] = zu[:, :CONV_WIDTH] * jax.nn.sigmoid(zu[:, CONV_WIDTH:])

    pin_ref[...] = _dot(xn, wpool_ref[...])


def _inproj(x2, norm, tabs, ws, kv_prev, *, layer, depth, tm, tiles_per_seq):
    n = x2.shape[0]
    grid = (n // tm,)
    row = lambda w: pl.BlockSpec((tm, w), lambda i: (i, 0))
    tab = pl.BlockSpec((tm, LANES), lambda i: (i % tiles_per_seq, 0))
    kv_shape = jax.ShapeDtypeStruct((depth, n * N_KV_HEADS, HEAD_DIM), _F32)
    if kv_prev is None:
        kv_spec = pl.BlockSpec((depth, tm * N_KV_HEADS, HEAD_DIM), lambda i: (0, i, 0))
    else:
        kv_spec = pl.BlockSpec((None, tm * N_KV_HEADS, HEAD_DIM), lambda i: (layer, i, 0))
    out_shape = [
        jax.ShapeDtypeStruct((n, ATT_WIDTH), _BF16),
        kv_shape,
        kv_shape,
        jax.ShapeDtypeStruct((n, KV_WIDTH), _BF16),
        jax.ShapeDtypeStruct((n, KV_WIDTH), _BF16),
        jax.ShapeDtypeStruct((n, IDX_WIDTH), _BF16),
        jax.ShapeDtypeStruct((n, IDX_DIM), _F32),
        jax.ShapeDtypeStruct((n, IDX_DIM), _BF16),
        jax.ShapeDtypeStruct((n, N_IDX_HEADS), _F32),
        jax.ShapeDtypeStruct((n, D_MODEL), _BF16),
        jax.ShapeDtypeStruct((n, CONV_WIDTH), _F32),
        jax.ShapeDtypeStruct((n, POOL_WIDTH), _F32),
    ]
    assert len(out_shape) == N_INPROJ_OUTS
    out_specs = [kv_spec if s is kv_shape else row(s.shape[1]) for s in out_shape]
    in_specs = ([row(D_MODEL), _const_spec((1, D_MODEL))] + [tab] * 5
                + [_const_spec(w.shape) for w in ws])
    args = (x2, norm, *tabs, *ws)
    aliases = {}
    if kv_prev is not None:
        aliases = {len(args): 1, len(args) + 1: 2}
        in_specs = in_specs + [pl.BlockSpec(memory_space=pl.ANY)] * 2
        args = args + tuple(kv_prev)
    return pl.pallas_call(
        functools.partial(_inproj_kernel, first_plane=layer if kv_prev is None else None),
        grid=grid, in_specs=in_specs, out_specs=out_specs, out_shape=out_shape,
        input_output_aliases=aliases,
        compiler_params=pltpu.CompilerParams(dimension_semantics=("arbitrary",),
                                             vmem_limit_bytes=VMEM_LIMIT),
        name="inproj",
    )(*args)


def _softmax_attention(q_ref, gate_ref, kb_ref, vb_ref, bias_s, o_ref, *, tq, sub, s_len):
    for r0 in range(0, tq, sub):
        rows = slice(r0, r0 + sub)
        q = q_ref[rows, :]
        gate = gate_ref[rows, :].astype(_F32)
        bias = bias_s[rows, :]
        for g in range(N_KV_HEADS):
            heads = [g * HEADS_PER_KV + r for r in range(HEADS_PER_KV)]
            qg = jnp.concatenate([q[:, h * HEAD_DIM:(h + 1) * HEAD_DIM] for h in heads], axis=0)
            s = _dot_nt(qg, kb_ref[:, g * HEAD_DIM:(g + 1) * HEAD_DIM])
            s = (s.reshape(HEADS_PER_KV, sub, s_len) + bias[None]).reshape(HEADS_PER_KV * sub, s_len)
            m = jnp.max(s, axis=-1, keepdims=True)
            p = jnp.exp(s - m)
            den = jnp.sum(p, axis=-1, keepdims=True)
            o = _dot(p.astype(_BF16), vb_ref[:, g * HEAD_DIM:(g + 1) * HEAD_DIM]) / den
            for r, h in enumerate(heads):
                cols = slice(h * HEAD_DIM, (h + 1) * HEAD_DIM)
                o_ref[rows, cols] = (o[r * sub:(r + 1) * sub] * gate[:, cols]).astype(_BF16)


def _visible(tq, cols, col0, q_pos0, n_valid):
    qpos = q_pos0 + lax.broadcasted_iota(jnp.int32, (tq, 1), 0)
    kpos = col0 + lax.broadcasted_iota(jnp.int32, (1, cols), 1)
    return (lax.div(kpos, jnp.int32(CHUNK)) <= lax.div(qpos, jnp.int32(CHUNK))) & (kpos < n_valid)


def _attn_dense_kernel(q_ref, gate_ref, kb_ref, vb_ref, o_ref, bias_s, *, tq, sub, s_len, q_pos0, n_valid):
    bias_s[...] = jnp.where(_visible(tq, s_len, 0, q_pos0, n_valid), 0.0, NEG_BIG)
    if o_ref.shape[0] > tq:
        o_ref[tq:, :] = jnp.zeros((o_ref.shape[0] - tq, o_ref.shape[1]), o_ref.dtype)
    _softmax_attention(q_ref, gate_ref, kb_ref, vb_ref, bias_s, o_ref, tq=tq, sub=sub, s_len=s_len)


def _attn_kernel(q_ref, gate_ref, kb_ref, vb_ref, qi0_ref, wi0_ref, kib0_ref, qi1_ref, wi1_ref, kib1_ref,
                 obuf_ref, o_ref, score_s, acc_s, bias_s, qi_s, w_s, *, nb, tq, sub, s_len, q_pos0, n_valid, topk):
    del obuf_ref
    b = pl.program_id(0)
    slot = lax.rem(b, 2)
    nslot = 1 - slot
    half = s_len // 2
    nblk = half // LANES
    n_slices = 2 * N_IDX_HEADS
    assert n_slices == 32
    tiles = range(nb)

    qpos = q_pos0 + lax.broadcasted_iota(jnp.int32, (tq, 1), 0)
    n_vis = jnp.minimum((lax.div(qpos, jnp.int32(CHUNK)) + 1) * CHUNK, n_valid)
    vis_half = lambda hf: _visible(tq, half, hf * half, q_pos0, n_valid)

    @pl.when(b == 0)
    def _():
        for bb in tiles:
            qi = qi0_ref[bb]
            wi = wi0_ref[bb]
            for hf in range(2):
                kib = kib0_ref[bb, hf * half:(hf + 1) * half, :]
                acc = jnp.zeros((tq, half), _F32)
                for h in range(N_IDX_HEADS):
                    l = _dot_nt(qi[:, h * IDX_DIM:(h + 1) * IDX_DIM], kib)
                    acc = acc + jnp.maximum(l, 0.0) * wi[:, h:h + 1]
                score_s[bb, 0, hf] = jnp.where(vis_half(hf), acc, -jnp.inf)

    for bb in tiles:
        qi1 = qi1_ref[bb]
        wi1 = wi1_ref[bb]
        for h in range(N_IDX_HEADS):
            qi_s[bb, h] = qi1[:, h * IDX_DIM:(h + 1) * IDX_DIM]
            w_s[bb, h] = jnp.broadcast_to(wi1[:, h:h + 1], (tq, LANES))
    acc_s[...] = jnp.zeros((nb, 2, tq, half), _F32)

    int_min = jnp.int32(-2 ** 31)

    def decode(u):
        sk = u ^ int_min
        bits = jnp.where(sk < 0, sk ^ jnp.int32(0x7FFFFFFF), sk)
        return lax.bitcast_convert_type(bits, _F32)

    def count(bb, pred):
        return (jnp.sum(pred(score_s[bb, slot, 0]).astype(_F32), axis=-1, keepdims=True)
                + jnp.sum(pred(score_s[bb, slot, 1]).astype(_F32), axis=-1, keepdims=True))

    def step(i, us):
        h = lax.div(i, jnp.int32(2))
        hf = lax.rem(i, jnp.int32(2))
        out = []
        for bb, u in zip(tiles, us):
            cand = u | lax.shift_left(jnp.int32(1), 31 - i)
            t = decode(cand)
            out.append(jnp.where(count(bb, lambda s: s >= t) >= topk, cand, u))
            l = _dot_nt(qi_s[bb, h], kib1_ref[bb, pl.ds(pl.multiple_of(hf * half, LANES), half), :])
            wt = w_s[bb, h]
            term = jnp.concatenate([jnp.maximum(l[:, j * LANES:(j + 1) * LANES], 0.0) * wt for j in range(nblk)],
                                   axis=1)
            acc_s[bb, hf] = acc_s[bb, hf] + term
        return tuple(out)

    us = lax.fori_loop(0, n_slices, step, tuple(jnp.zeros((tq, 1), jnp.int32) for _ in tiles))
    thrs, flags = [], []
    for bb, u in zip(tiles, us):
        for hf in range(2):
            score_s[bb, nslot, hf] = jnp.where(vis_half(hf), acc_s[bb, hf], -jnp.inf)
        thr = jnp.where(n_vis <= topk, -jnp.inf, decode(u))
        n_ge = count(bb, lambda s: s >= thr)
        for hf in range(2):
            keep = (score_s[bb, slot, hf] >= thr) & vis_half(hf)
            bias_s[bb, :, hf * half:(hf + 1) * half] = jnp.where(keep, 0.0, NEG_BIG)
        thrs.append(thr)
        flags.append(jnp.max(jnp.where((n_ge > topk) & (n_vis > topk), 1.0, 0.0)))

    @pl.when(functools.reduce(jnp.maximum, flags) > 0.0)
    def _():
        earlier = (lax.broadcasted_iota(jnp.int32, (LANES, LANES), 0)
                   < lax.broadcasted_iota(jnp.int32, (LANES, LANES), 1)).astype(_BF16)
        for bb, thr in zip(tiles, thrs):
            need = topk - count(bb, lambda s: s > thr)
            run = jnp.zeros((tq, 1), _F32)
            for hf in range(2):
                for j in range(nblk):
                    c0 = hf * half + j * LANES
                    sc = score_s[bb, slot, hf, :, j * LANES:(j + 1) * LANES]
                    vis = _visible(tq, LANES, c0, q_pos0, n_valid)
                    eq = (sc == thr) & vis
                    e = eq.astype(_F32)
                    rank = _dot(e.astype(_BF16), earlier) + run
                    pick = ((sc > thr) & vis) | (eq & (rank < need))
                    bias_s[bb, :, c0:c0 + LANES] = jnp.where(pick, 0.0, NEG_BIG)
                    run = run + jnp.sum(e, axis=-1, keepdims=True)

    for bb in tiles:
        _softmax_attention(q_ref.at[bb], gate_ref.at[bb], kb_ref.at[bb], vb_ref.at[bb], bias_s.at[bb], o_ref.at[bb],
                           tq=tq, sub=sub, s_len=s_len)


def _attention_tile(obuf, q, qi, wi, gates, kb, vb, kib, *, tq, qblk, s_len, q_pos0, n_valid, topk):
    b, t, _ = q.shape
    sub = min(tq, ATTN_SUB_LONG if s_len >= ATTN_LONG_KEYS else ATTN_SUB)
    params = pltpu.CompilerParams(dimension_semantics=("arbitrary",), vmem_limit_bytes=VMEM_LIMIT)
    cur = lambda w: pl.BlockSpec((None, tq, w), lambda i: (i, qblk, 0))
    kcur = lambda w: pl.BlockSpec((None, s_len, w), lambda i: (i, 0, 0))
    out_shape = jax.ShapeDtypeStruct((b, t, ATT_WIDTH), _BF16)
    if s_len <= topk:
        assert obuf is None
        kern = functools.partial(_attn_dense_kernel, tq=tq, sub=sub, s_len=s_len, q_pos0=q_pos0, n_valid=n_valid)
        return pl.pallas_call(
            kern, grid=(b,), in_specs=[cur(ATT_WIDTH), cur(ATT_WIDTH), kcur(KV_WIDTH), kcur(KV_WIDTH)],
            out_specs=pl.BlockSpec((None, t, ATT_WIDTH), lambda i: (i, 0, 0)), out_shape=out_shape,
            scratch_shapes=[pltpu.VMEM((tq, s_len), _F32)], compiler_params=params, name="attention_dense",
        )(q, gates, kb, vb)
    if obuf is None:
        obuf = jnp.zeros(out_shape.shape, _BF16)
    nb = 1
    while (nb < ATTN_MAX_CHAINS and b % (2 * nb) == 0 and 2 * nb * tq <= ATTN_STEP_ROWS
           and 2 * nb * tq * s_len <= ATTN_STEP_ELEMS):
        nb *= 2
    steps = b // nb
    ahead = lambda i: jnp.minimum(i + 1, steps - 1)
    cur = lambda w: pl.BlockSpec((nb, tq, w), lambda i: (i, qblk, 0))
    kcur = lambda w: pl.BlockSpec((nb, s_len, w), lambda i: (i, 0, 0))
    nxt = lambda w: pl.BlockSpec((nb, tq, w), lambda i: (ahead(i), qblk, 0))
    knxt = lambda w: pl.BlockSpec((nb, s_len, w), lambda i: (ahead(i), 0, 0))
    kern = functools.partial(_attn_kernel, nb=nb, tq=tq, sub=sub, s_len=s_len, q_pos0=q_pos0, n_valid=n_valid,
                             topk=topk)
    half = s_len // 2
    return pl.pallas_call(
        kern, grid=(steps,),
        in_specs=[cur(ATT_WIDTH), cur(ATT_WIDTH), kcur(KV_WIDTH), kcur(KV_WIDTH),
                  cur(IDX_WIDTH), cur(N_IDX_HEADS), kcur(IDX_DIM),
                  nxt(IDX_WIDTH), nxt(N_IDX_HEADS), knxt(IDX_DIM),
                  pl.BlockSpec(memory_space=pl.ANY)],
        out_specs=cur(ATT_WIDTH), out_shape=out_shape, input_output_aliases={10: 0},
        scratch_shapes=[pltpu.VMEM((nb, 2, 2, tq, half), _F32), pltpu.VMEM((nb, 2, tq, half), _F32),
                        pltpu.VMEM((nb, tq, s_len), _F32),
                        pltpu.VMEM((nb, N_IDX_HEADS, tq, IDX_DIM), _BF16),
                        pltpu.VMEM((nb, N_IDX_HEADS, tq, LANES), _F32)],
        compiler_params=params, name="attention",
    )(q, gates, kb, vb, qi, wi, kib, qi, wi, kib, obuf)


def _attention(q, qi, wi, gates, kb, vb, kib, *, q_pos0, n_valid, topk, causal_tiles):
    b, t, _ = q.shape
    if causal_tiles:
        tq = _row_tile(t, ATTN_TQ)
        assert q_pos0 == 0 and tq % (2 * LANES) == 0
        obuf = None
        for j in range(t // tq):
            obuf = _attention_tile(obuf, q, qi, wi, gates, kb, vb, kib, tq=tq, qblk=j, s_len=(j + 1) * tq,
                                   q_pos0=j * tq, n_valid=n_valid, topk=topk)
        return obuf
    return _attention_tile(None, q, qi, wi, gates, kb, vb, kib, tq=t, qblk=0, s_len=kb.shape[1],
                           q_pos0=q_pos0, n_valid=n_valid, topk=topk)


def _mix_kernel(u_ref, uprev_ref, ust_ref, pin_ref, pprev_ref, pst_ref, gconv_ref, gpool_ref,
                cw_ref, cb_ref, lng_ref, lnb_ref, pw_ref, poolw_ref, pscale_ref,
                oc_ref, op_ref, xp_s, pp_s, sh_s, *, tt, q_pos0):
    t = pl.program_id(1)

    @pl.when(t == 0)
    def _():
        xp_s[0:CONV_HALO, :] = ust_ref[...]
        pp_s[0:POOL_HALO, :] = pst_ref[...]

    @pl.when(t > 0)
    def _():
        xp_s[0:CONV_HALO, :] = uprev_ref[...]
        pp_s[0:POOL_HALO, :] = pprev_ref[...]

    xp_s[CONV_HALO:CONV_HALO + tt, :] = u_ref[...]
    pin = pin_ref[...]
    pp_s[POOL_HALO:POOL_HALO + tt, :] = pin

    first = CONV_HALO - (CONV_K - 1)
    span = tt + CONV_HALO - SUBLANES
    acc = jnp.zeros((tt, CONV_WIDTH), _F32)
    for r in range(SUBLANES):
        if r:
            sh_s[r - 1] = xp_s[pl.ds(r, span), :]
        for j in range(CONV_K):
            if (first + j) % SUBLANES == r:
                base = first + j - r
                rows = sh_s[r - 1, pl.ds(base, tt), :] if r else xp_s[pl.ds(base, tt), :]
                acc = acc + rows * cw_ref[j:j + 1, :]
    y = acc + cb_ref[...]
    mu = jnp.mean(y, axis=-1, keepdims=True)
    yc = y - mu
    var = jnp.mean(yc * yc, axis=-1, keepdims=True)
    yn = yc * lax.rsqrt(var + EPS) * lng_ref[...] + lnb_ref[...]
    act = (yn * jax.nn.sigmoid(yn)).astype(_BF16)
    oc_ref[...] = (_dot(act, pw_ref[...]) * gconv_ref[...].astype(_F32)).astype(_BF16)

    pos = q_pos0 + t * tt + lax.broadcasted_iota(jnp.int32, (tt, 1), 0)
    for gi, w in enumerate(POOL_WINDOWS):
        cols = slice(gi * POOL_GROUP, (gi + 1) * POOL_GROUP)
        win = pp_s[pl.ds(POOL_HALO, tt), cols]
        for i in range(1, w):
            win = win + pp_s[pl.ds(POOL_HALO - i, tt), cols]
        cnt = jnp.minimum(pos + 1, w).astype(_F32)
        r = (win / cnt - pin[:, cols]).astype(_BF16)
        yp = _dot(r, poolw_ref[gi]) * pscale_ref[:, cols]
        op_ref[:, cols] = (yp * gpool_ref[:, cols].astype(_F32)).astype(_BF16)


def _mixers(u, ust, pin, pst, gates, cw, cb, lng, lnb, pw, poolw, pscale, *, tt, q_pos0):
    b, t, _ = u.shape
    grid = (b, t // tt)
    cur = lambda w: pl.BlockSpec((None, tt, w), lambda i, j: (i, j, 0))

    def prev(rows, src_rows):
        if src_rows < rows:
            return None
        per = tt // rows
        return pl.BlockSpec((None, rows, CONV_WIDTH), lambda i, j: (i, jnp.maximum(j * per - 1, 0), 0))

    uprev_spec = prev(CONV_HALO, t)
    pprev_spec = prev(POOL_HALO, t)
    state = lambda rows: pl.BlockSpec((None, rows, CONV_WIDTH), lambda i, j: (i, 0, 0))
    uprev_src, pprev_src = u, pin
    if uprev_spec is None:
        assert t == tt
        uprev_src, uprev_spec = ust, state(CONV_HALO)
    if pprev_spec is None:
        assert t == tt
        pprev_src, pprev_spec = pst, state(POOL_HALO)
    gate_col = lambda blk: pl.BlockSpec((None, tt, CONV_WIDTH), lambda i, j: (i, j, blk))
    kern = functools.partial(_mix_kernel, tt=tt, q_pos0=q_pos0)
    return pl.pallas_call(
        kern, grid=grid,
        in_specs=[cur(CONV_WIDTH), uprev_spec, state(CONV_HALO), cur(POOL_WIDTH), pprev_spec, state(POOL_HALO),
                  gate_col(ATT_WIDTH // CONV_WIDTH), gate_col(ATT_WIDTH // CONV_WIDTH + 1),
                  _const_spec(cw.shape), _const_spec(cb.shape), _const_spec(lng.shape), _const_spec(lnb.shape),
                  _const_spec(pw.shape), _const_spec(poolw.shape), _const_spec(pscale.shape)],
        out_specs=[cur(CONV_WIDTH), cur(POOL_WIDTH)],
        out_shape=[jax.ShapeDtypeStruct((b, t, CONV_WIDTH), _BF16), jax.ShapeDtypeStruct((b, t, POOL_WIDTH), _BF16)],
        scratch_shapes=[pltpu.VMEM((CONV_HALO + tt, CONV_WIDTH), _F32), pltpu.VMEM((POOL_HALO + tt, POOL_WIDTH), _F32),
                        pltpu.VMEM((SUBLANES - 1, tt + CONV_HALO - SUBLANES, CONV_WIDTH), _F32)],
        compiler_params=pltpu.CompilerParams(dimension_semantics=("arbitrary", "arbitrary"),
                                             vmem_limit_bytes=VMEM_LIMIT),
        name="mixers",
    )(u, uprev_src, ust, pin, pprev_src, pst, gates, gates, cw, cb, lng, lnb, pw, poolw, pscale)


def _outproj_kernel(x_ref, a_ref, c_ref, pl_ref, p_ref, woa_ref, woc_ref, wop_ref, nple_ref,
                    wgate_ref, wproj_ref, nfin_ref, o_ref, *, final):
    x1 = (x_ref[...] + _dot(a_ref[...], woa_ref[...]) + _dot(c_ref[...], woc_ref[...])
          + _dot(pl_ref[...], wop_ref[...]))
    xn = _rms(x1, nple_ref[...]).astype(_BF16)
    gate = jax.nn.sigmoid(_dot(xn, wgate_ref[...]))
    x2 = x1 + gate * _dot(p_ref[...].astype(_BF16), wproj_ref[...])
    o_ref[...] = _rms(x2, nfin_ref[...]) if final else x2


def _outproj(x2, a, c, po, p_all, layer, w_out, nple, wgate, wproj, nfin, *, tm, final):
    n = x2.shape[0]
    row = lambda w: pl.BlockSpec((tm, w), lambda i: (i, 0))
    wblk = lambda rows, blk: pl.BlockSpec((rows, D_MODEL), lambda i: (blk, 0), pipeline_mode=pl.Buffered(1))
    p_spec = pl.BlockSpec((None, tm, PLE_DIM), lambda i: (layer, i, 0))
    return pl.pallas_call(
        functools.partial(_outproj_kernel, final=final), grid=(n // tm,),
        in_specs=[row(D_MODEL), row(ATT_WIDTH), row(CONV_WIDTH), row(POOL_WIDTH), p_spec,
                  wblk(ATT_WIDTH, 0), wblk(CONV_WIDTH, ATT_WIDTH // CONV_WIDTH),
                  wblk(POOL_WIDTH, ATT_WIDTH // CONV_WIDTH + 1),
                  _const_spec(nple.shape), _const_spec(wgate.shape), _const_spec(wproj.shape),
                  _const_spec(nfin.shape)],
        out_specs=row(D_MODEL),
        out_shape=jax.ShapeDtypeStruct((n, D_MODEL), _F32),
        compiler_params=pltpu.CompilerParams(dimension_semantics=("arbitrary",),
                                             vmem_limit_bytes=VMEM_LIMIT),
        name="outproj",
    )(x2, a, c, po, p_all, w_out, w_out, w_out, nple, wgate, wproj, nfin)


def _rope_tables(pos):
    pos = pos.astype(_F32)[:, None]

    def cs(half):
        freq = ROPE_THETA ** (-jnp.arange(half, dtype=_F32) / half)
        ang = pos * freq[None, :]
        return jnp.cos(ang), jnp.sin(ang)

    c, s = cs(HEAD_DIM // 2)
    c128 = jnp.concatenate([c, c], axis=-1)
    s128 = jnp.concatenate([-s, s], axis=-1)
    c, s = cs(IDX_DIM // 2)
    z = jnp.zeros_like(s)
    c64 = jnp.concatenate([c, c, c, c], axis=-1)
    s64a = jnp.concatenate([-s, z, -s, z], axis=-1)
    s64b = jnp.concatenate([z, s, z, s], axis=-1)
    return c128, s128, c64, s64a, s64b


def _split_w_in(w):
    w = w.astype(_BF16)
    cols = lambda a, n: w[:, a:a + n]
    pad = jnp.zeros((w.shape[0], LANES - IDX_DIM - N_IDX_HEADS), _BF16)
    wqk = jnp.concatenate([cols(_OFF_Q, ATT_WIDTH), cols(_OFF_K, KV_WIDTH)], axis=1)
    wv = cols(_OFF_V, KV_WIDTH)
    widx = jnp.concatenate([cols(_OFF_QI, IDX_WIDTH), cols(_OFF_KI, IDX_DIM), cols(_OFF_WI, N_IDX_HEADS), pad], axis=1)
    wgate = jnp.concatenate([cols(_OFF_GATT, ATT_WIDTH), cols(_OFF_GCONV, CONV_WIDTH), cols(_OFF_GPOOL, POOL_WIDTH)], axis=1)
    wglu = cols(_OFF_GLU, 2 * CONV_WIDTH)
    wpool = cols(_OFF_POOL, POOL_WIDTH)
    return wqk, wv, widx, wgate, wglu, wpool


def _row_tile(n, pref):
    t = min(pref, n)
    while n % t:
        t //= 2
    return t


def _layer(x, p_all, kv_prev, layer, pos0, k_past, v_past, ki_past, conv_prev, pool_prev, topk, lw, nfin, final):
    b, t, _ = x.shape
    n = b * t
    (norm_mix, ws, cw, cb, lng, lnb, pw, poolw, pscale, w_out, nple, wgate, wproj) = lw

    tm = _row_tile(t, 256) if b > 1 and t >= 256 else n
    if tm <= t:
        tabs = _rope_tables(pos0 + jnp.arange(t))
        tiles_per_seq = t // tm
    else:
        tabs = tuple(jnp.tile(a, (b, 1)) for a in _rope_tables(pos0 + jnp.arange(t)))
        tiles_per_seq = 1
    (q, k_all, v_all, kb, vb, qi, ki, kib, wi, gates, u, pin) = _inproj(
        x.reshape(n, D_MODEL), norm_mix, tabs, ws, kv_prev, layer=layer, depth=p_all.shape[0], tm=tm,
        tiles_per_seq=tiles_per_seq)
    r3 = lambda a: a.reshape(b, t, a.shape[-1])

    kb3, vb3, kib3 = r3(kb), r3(vb), r3(kib)
    if k_past is not None:
        past = k_past.shape[1]
        kb3 = jnp.concatenate([k_past.reshape(b, past, KV_WIDTH).astype(_BF16), kb3], axis=1)
        vb3 = jnp.concatenate([v_past.reshape(b, past, KV_WIDTH).astype(_BF16), vb3], axis=1)
        kib3 = jnp.concatenate([ki_past.astype(_BF16), kib3], axis=1)
    n_valid = kb3.shape[1]
    s_pad = -(-n_valid // (2 * LANES)) * (2 * LANES)
    if s_pad != n_valid:
        padk = lambda a: jnp.pad(a, ((0, 0), (0, s_pad - n_valid), (0, 0)))
        kb3, vb3, kib3 = padk(kb3), padk(vb3), padk(kib3)
    o_att = _attention(r3(q), r3(qi), r3(wi), r3(gates), kb3, vb3, kib3, q_pos0=pos0, n_valid=n_valid,
                       topk=topk, causal_tiles=k_past is None and t % (2 * LANES) == 0)

    u3, pin3 = r3(u), r3(pin)
    ust = jnp.pad(conv_prev, ((0, 0), (CONV_HALO - (CONV_K - 1), 0), (0, 0)))
    pst = jnp.pad(pool_prev, ((0, 0), (POOL_HALO - POOL_HIST, 0), (0, 0)))
    tt = _row_tile(t, 256)
    oc, op = _mixers(u3, ust, pin3, pst, r3(gates), cw, cb, lng, lnb, pw, poolw, pscale, tt=tt, q_pos0=pos0)

    tm2 = _row_tile(n, 256)
    x_new = _outproj(x.reshape(n, D_MODEL), o_att.reshape(n, ATT_WIDTH), oc.reshape(n, CONV_WIDTH),
                     op.reshape(n, POOL_WIDTH), p_all.reshape(-1, n, PLE_DIM), layer, w_out, nple, wgate, wproj,
                     nfin, tm=tm2, final=final).reshape(b, t, D_MODEL)

    conv_state = jnp.concatenate([conv_prev, u3], axis=1)[:, -(CONV_K - 1):]
    pool_state = jnp.concatenate([pool_prev, pin3], axis=1)[:, -POOL_HIST:]
    return x_new, (k_all, v_all), r3(ki), conv_state, pool_state


def kernel(x_prompt, x_sample, p_prompt, p_sample, cache_k, cache_v, cache_kidx, state_conv, state_pool, norm_mix, w_in, conv_w, conv_b, conv_ln_g, conv_ln_b, conv_pw, pool_w, pool_scale, w_out, norm_ple, w_ple_gate, w_ple_proj, norm_final):
    bp, tp, _ = x_prompt.shape
    bs, ts, _ = x_sample.shape
    depth = w_in.shape[0]
    past = cache_k.shape[2]
    topk_p = min(TOPK_MAX, tp // 4)
    topk_s = min(TOPK_MAX, (past + ts) // 4)
    conv0 = jnp.zeros((bp, CONV_K - 1, CONV_WIDTH), x_prompt.dtype)
    pool0 = jnp.zeros((bp, POOL_HIST, POOL_WIDTH), x_prompt.dtype)
    nfin = norm_final.reshape(1, D_MODEL)
    hp, hs = x_prompt, x_sample
    kv_p = kv_s = None
    outs_p, outs_s = [], []
    for i in range(depth):
        vec = lambda a: a[i].reshape(1, -1)
        lw = (vec(norm_mix), _split_w_in(w_in[i]), conv_w[i], vec(conv_b), vec(conv_ln_g), vec(conv_ln_b),
              conv_pw[i].astype(_BF16), pool_w[i].astype(_BF16), vec(pool_scale), w_out[i].astype(_BF16),
              vec(norm_ple), w_ple_gate[i].astype(_BF16), w_ple_proj[i].astype(_BF16))
        final = i == depth - 1
        hp, kv_p, *op_ = _layer(hp, p_prompt, kv_p, i, 0, None, None, None, conv0, pool0, topk_p, lw, nfin, final)
        hs, kv_s, *os_ = _layer(hs, p_sample, kv_s, i, past, cache_k[i], cache_v[i], cache_kidx[i],
                                state_conv[i], state_pool[i], topk_s, lw, nfin, final)
        outs_p.append(op_)
        outs_s.append(os_)
    stack = lambda outs, j: jnp.stack([o[j] for o in outs])
    heads = lambda a, b, t: a.reshape(depth, b, t, N_KV_HEADS, HEAD_DIM)
    return ((hp, hs, heads(kv_p[0], bp, tp), heads(kv_p[1], bp, tp)) + tuple(stack(outs_p, j) for j in range(3))
            + (heads(kv_s[0], bs, ts), heads(kv_s[1], bs, ts)) + tuple(stack(outs_s, j) for j in range(3)))
```

```python
import functools

import numpy as np
import jax
import jax.numpy as jnp
from jax import lax
from jax.experimental import pallas as pl
from jax.experimental.pallas import tpu as pltpu

D_MODEL = 2048
CHUNK = 64
HEAD_DIM = 128
ATT_WIDTH = D_MODEL // 2
CONV_WIDTH = D_MODEL // 4
POOL_WIDTH = D_MODEL // 4
N_HEADS = ATT_WIDTH // HEAD_DIM
N_KV_HEADS = 2
HEADS_PER_KV = N_HEADS // N_KV_HEADS
KV_WIDTH = N_KV_HEADS * HEAD_DIM
N_IDX_HEADS = 16
IDX_DIM = 64
IDX_WIDTH = N_IDX_HEADS * IDX_DIM
TOPK_MAX = 256
ROPE_THETA = 10000.0
CONV_K = 31
CONV_HALO = 32
POOL_WINDOWS = (2, 4, 8, 16)
POOL_GROUP = POOL_WIDTH // len(POOL_WINDOWS)
POOL_HIST = 15
POOL_HALO = 16
PLE_DIM = 256
EPS = 1e-6
LANES = 128
SUBLANES = 8
NEG_BIG = -1e30
VMEM_LIMIT = 60 * 1024 * 1024
N_INPROJ_OUTS = 12
N_MIX_IN = 9
N_MIX_TAIL = 5
ATTN_TQ = 256
ATTN_SUB = 128
ATTN_STEP_ELEMS = 2 * 256 * 768
ATTN_STEP_ROWS = 512
ATTN_MAX_CHAINS = 8

_OFF_Q = 0
_OFF_K = _OFF_Q + ATT_WIDTH
_OFF_V = _OFF_K + KV_WIDTH
_OFF_QI = _OFF_V + KV_WIDTH
_OFF_KI = _OFF_QI + IDX_WIDTH
_OFF_WI = _OFF_KI + IDX_DIM
_OFF_GATT = _OFF_WI + N_IDX_HEADS
_OFF_GLU = _OFF_GATT + ATT_WIDTH
_OFF_GCONV = _OFF_GLU + 2 * CONV_WIDTH
_OFF_POOL = _OFF_GCONV + CONV_WIDTH
_OFF_GPOOL = _OFF_POOL + POOL_WIDTH
N_IN = _OFF_GPOOL + POOL_WIDTH

_F32 = jnp.float32
_BF16 = jnp.bfloat16


def _const_spec(shape):
    nd = len(shape)
    return pl.BlockSpec(shape, lambda *_: (0,) * nd, pipeline_mode=pl.Buffered(1))


def _dot(a, b):
    return jnp.dot(a, b, preferred_element_type=_F32)


def _dot_nt(a, b):
    return lax.dot_general(a, b, (((1,), (1,)), ((), ())), preferred_element_type=_F32)


def _rms(x, g):
    return x * lax.rsqrt(jnp.mean(x * x, axis=-1, keepdims=True) + EPS) * g


def _inproj_kernel(x_ref, g_ref, c128_ref, s128_ref, c64_ref, s64a_ref, s64b_ref,
                   wqk_ref, wv_ref, widx_ref, wgate_ref, wglu_ref, wpool_ref, *rest, first_plane, mix):
    tm = x_ref.shape[0]
    if mix is not None:
        (ust_ref, pst_ref, cw_ref, cb_ref, lng_ref, lnb_ref, pw_ref, poolw_ref, pscale_ref) = rest[:N_MIX_IN]
        oc_ref, op_ref, xp_s, pp_s, sh_s = rest[-N_MIX_TAIL:]
        rest = rest[N_MIX_IN:-N_MIX_TAIL]
        t = lax.rem(pl.program_id(0), mix[1])

        @pl.when(t == 0)
        def _():
            xp_s[0:CONV_HALO, :] = ust_ref[...]
            pp_s[0:POOL_HALO, :] = pst_ref[...]

        @pl.when(t > 0)
        def _():
            xp_s[0:CONV_HALO, :] = xp_s[tm:tm + CONV_HALO, :]
            pp_s[0:POOL_HALO, :] = pp_s[tm:tm + POOL_HALO, :]

    (q_ref, k_ref, v_ref, kb_ref, vb_ref, qi_ref, ki_ref, kib_ref, wi_ref,
     gates_ref, u_ref, pin_ref) = rest[-N_INPROJ_OUTS:]
    if first_plane is not None:
        for ref in (k_ref, v_ref):
            for d in range(ref.shape[0]):
                if d != first_plane:
                    ref[d] = jnp.zeros(ref.shape[1:], _F32)
        k_ref, v_ref = k_ref.at[first_plane], v_ref.at[first_plane]
    xn = _rms(x_ref[...], g_ref[...]).astype(_BF16)

    zg = _dot(xn, wgate_ref[...])
    gates = zg * jax.nn.sigmoid(zg)
    gates_ref[...] = gates.astype(_BF16)

    zu = _dot(xn, wglu_ref[...])
    u = zu[:, :CONV_WIDTH] * jax.nn.sigmoid(zu[:, CONV_WIDTH:])
    pin = _dot(xn, wpool_ref[...])
    if mix is None:
        u_ref[...] = u
        pin_ref[...] = pin
    else:
        pos = mix[0] + t * tm + lax.broadcasted_iota(jnp.int32, (tm, 1), 0)
        oc, op = _mix_compute(u, pin, gates[:, ATT_WIDTH:ATT_WIDTH + CONV_WIDTH], gates[:, ATT_WIDTH + CONV_WIDTH:],
                              pos, cw_ref, cb_ref, lng_ref, lnb_ref, pw_ref, poolw_ref, pscale_ref, xp_s, pp_s, sh_s)
        oc_ref[...] = oc
        op_ref[...] = op
        u_ref[...] = u[tm - CONV_HALO:, :]
        pin_ref[...] = pin[tm - POOL_HALO:, :]

    zqk = _dot(xn, wqk_ref[...])
    c128 = c128_ref[...]
    s128 = s128_ref[...]
    for h in range(N_HEADS + N_KV_HEADS):
        xh = zqk[:, h * HEAD_DIM:(h + 1) * HEAD_DIM]
        r = xh * c128 + pltpu.roll(xh, HEAD_DIM // 2, 1) * s128
        if h < N_HEADS:
            q_ref[:, h * HEAD_DIM:(h + 1) * HEAD_DIM] = (r * (HEAD_DIM ** -0.5)).astype(_BF16)
        else:
            j = h - N_HEADS
            k_ref[pl.ds(j, tm, stride=N_KV_HEADS), :] = r
            kb_ref[:, j * HEAD_DIM:(j + 1) * HEAD_DIM] = r.astype(_BF16)

    zv = _dot(xn, wv_ref[...])
    for j in range(N_KV_HEADS):
        v_ref[pl.ds(j, tm, stride=N_KV_HEADS), :] = zv[:, j * HEAD_DIM:(j + 1) * HEAD_DIM]
    vb_ref[...] = zv.astype(_BF16)

    zi = _dot(xn, widx_ref[...])
    c64 = c64_ref[...]
    s64a = s64a_ref[...]
    s64b = s64b_ref[...]
    n_groups = IDX_WIDTH // LANES
    for j in range(n_groups + 1):
        xj = zi[:, j * LANES:(j + 1) * LANES]
        r = (xj * c64 + pltpu.roll(xj, LANES - IDX_DIM // 2, 1) * s64a
             + pltpu.roll(xj, IDX_DIM // 2, 1) * s64b)
        if j < n_groups:
            qi_ref[:, j * LANES:(j + 1) * LANES] = r.astype(_BF16)
        else:
            ki_ref[...] = r[:, :IDX_DIM]
            kib_ref[...] = r[:, :IDX_DIM].astype(_BF16)
            wi_ref[...] = xj[:, IDX_DIM:IDX_DIM + N_IDX_HEADS] * (IDX_DIM ** -0.5 * N_IDX_HEADS ** -0.5)


def _inproj(x2, norm, tabs, ws, kv_prev, mix_ops, *, layer, depth, tm, tiles_per_seq, q_pos0):
    n = x2.shape[0]
    grid = (n // tm,)
    row = lambda w: pl.BlockSpec((tm, w), lambda i: (i, 0))
    tab = pl.BlockSpec((tm, LANES), lambda i: (i % tiles_per_seq, 0))
    kv_shape = jax.ShapeDtypeStruct((depth, n * N_KV_HEADS, HEAD_DIM), _F32)
    if kv_prev is None:
        kv_spec = pl.BlockSpec((depth, tm * N_KV_HEADS, HEAD_DIM), lambda i: (0, i, 0))
    else:
        kv_spec = pl.BlockSpec((None, tm * N_KV_HEADS, HEAD_DIM), lambda i: (layer, i, 0))
    out_shape = [
        jax.ShapeDtypeStruct((n, ATT_WIDTH), _BF16),
        kv_shape,
        kv_shape,
        jax.ShapeDtypeStruct((n, KV_WIDTH), _BF16),
        jax.ShapeDtypeStruct((n, KV_WIDTH), _BF16),
        jax.ShapeDtypeStruct((n, IDX_WIDTH), _BF16),
        jax.ShapeDtypeStruct((n, IDX_DIM), _F32),
        jax.ShapeDtypeStruct((n, IDX_DIM), _BF16),
        jax.ShapeDtypeStruct((n, N_IDX_HEADS), _F32),
        jax.ShapeDtypeStruct((n, D_MODEL), _BF16),
        jax.ShapeDtypeStruct((n, CONV_WIDTH), _F32),
        jax.ShapeDtypeStruct((n, POOL_WIDTH), _F32),
    ]
    assert len(out_shape) == N_INPROJ_OUTS
    out_specs = [kv_spec if s is kv_shape else row(s.shape[1]) for s in out_shape]
    in_specs = ([row(D_MODEL), _const_spec((1, D_MODEL))] + [tab] * 5
                + [_const_spec(w.shape) for w in ws])
    args = (x2, norm, *tabs, *ws)
    scratch = []
    if mix_ops is not None:
        assert len(mix_ops) == N_MIX_IN and tm >= CONV_HALO
        n_seq = n // (tm * tiles_per_seq)
        seq = lambda rows: pl.BlockSpec((None, rows, CONV_WIDTH), lambda i: (i // tiles_per_seq, 0, 0))
        in_specs += [seq(CONV_HALO), seq(POOL_HALO)] + [_const_spec(w.shape) for w in mix_ops[2:]]
        args += tuple(mix_ops)
        out_shape[-2:] = [jax.ShapeDtypeStruct((n_seq, CONV_HALO, CONV_WIDTH), _F32),
                          jax.ShapeDtypeStruct((n_seq, POOL_HALO, POOL_WIDTH), _F32)]
        out_specs[-2:] = [seq(CONV_HALO), seq(POOL_HALO)]
        out_shape += [jax.ShapeDtypeStruct((n, CONV_WIDTH), _BF16), jax.ShapeDtypeStruct((n, POOL_WIDTH), _BF16)]
        out_specs += [row(CONV_WIDTH), row(POOL_WIDTH)]
        scratch = [pltpu.VMEM((CONV_HALO + tm, CONV_WIDTH), _F32), pltpu.VMEM((POOL_HALO + tm, POOL_WIDTH), _F32),
                   pltpu.VMEM((SUBLANES - 1, tm + CONV_HALO - SUBLANES, CONV_WIDTH), _F32)]
        assert len(scratch) + 2 == N_MIX_TAIL
    aliases = {}
    if kv_prev is not None:
        aliases = {len(args): 1, len(args) + 1: 2}
        in_specs = in_specs + [pl.BlockSpec(memory_space=pl.ANY)] * 2
        args = args + tuple(kv_prev)
    return pl.pallas_call(
        functools.partial(_inproj_kernel, first_plane=layer if kv_prev is None else None,
                          mix=None if mix_ops is None else (q_pos0, tiles_per_seq)),
        grid=grid, in_specs=in_specs, out_specs=out_specs, out_shape=out_shape, scratch_shapes=scratch,
        input_output_aliases=aliases,
        compiler_params=pltpu.CompilerParams(dimension_semantics=("arbitrary",),
                                             vmem_limit_bytes=VMEM_LIMIT),
        name="inproj",
    )(*args)


def _softmax_attention(q_ref, gate_ref, kb_ref, vb_ref, bias_s, o_ref, *, tq, sub, s_len):
    for r0 in range(0, tq, sub):
        rows = slice(r0, r0 + sub)
        q = q_ref[rows, :]
        gate = gate_ref[rows, :].astype(_F32)
        bias = bias_s[rows, :]
        for g in range(N_KV_HEADS):
            heads = [g * HEADS_PER_KV + r for r in range(HEADS_PER_KV)]
            qg = jnp.concatenate([q[:, h * HEAD_DIM:(h + 1) * HEAD_DIM] for h in heads], axis=0)
            s = _dot_nt(qg, kb_ref[:, g * HEAD_DIM:(g + 1) * HEAD_DIM])
            s = (s.reshape(HEADS_PER_KV, sub, s_len) + bias[None]).reshape(HEADS_PER_KV * sub, s_len)
            m = jnp.max(s, axis=-1, keepdims=True)
            p = jnp.exp(s - m)
            den = jnp.sum(p, axis=-1, keepdims=True)
            o = _dot(p.astype(_BF16), vb_ref[:, g * HEAD_DIM:(g + 1) * HEAD_DIM]) / den
            for r, h in enumerate(heads):
                cols = slice(h * HEAD_DIM, (h + 1) * HEAD_DIM)
                o_ref[rows, cols] = (o[r * sub:(r + 1) * sub] * gate[:, cols]).astype(_BF16)


def _visible(tq, cols, col0, q_pos0, n_valid):
    qpos = q_pos0 + lax.broadcasted_iota(jnp.int32, (tq, 1), 0)
    kpos = col0 + lax.broadcasted_iota(jnp.int32, (1, cols), 1)
    return (lax.div(kpos, jnp.int32(CHUNK)) <= lax.div(qpos, jnp.int32(CHUNK))) & (kpos < n_valid)


def _attn_dense_kernel(q_ref, gate_ref, kb_ref, vb_ref, o_ref, bias_s, *, tq, sub, s_len, q_pos0, n_valid):
    bias_s[...] = jnp.where(_visible(tq, s_len, 0, q_pos0, n_valid), 0.0, NEG_BIG)
    if o_ref.shape[0] > tq:
        o_ref[tq:, :] = jnp.zeros((o_ref.shape[0] - tq, o_ref.shape[1]), o_ref.dtype)
    _softmax_attention(q_ref, gate_ref, kb_ref, vb_ref, bias_s, o_ref, tq=tq, sub=sub, s_len=s_len)


def _attn_kernel(q_ref, gate_ref, kb_ref, vb_ref, qi0_ref, wi0_ref, kib0_ref, qi1_ref, wi1_ref, kib1_ref,
                 obuf_ref, o_ref, score_s, acc_s, bias_s, qi_s, w_s, *, nb, tq, sub, s_len, q_pos0, n_valid, topk):
    del obuf_ref
    b = pl.program_id(0)
    slot = lax.rem(b, 2)
    nslot = 1 - slot
    half = s_len // 2
    nblk = half // LANES
    n_slices = 2 * N_IDX_HEADS
    assert n_slices == 32
    tiles = range(nb)

    qpos = q_pos0 + lax.broadcasted_iota(jnp.int32, (tq, 1), 0)
    n_vis = jnp.minimum((lax.div(qpos, jnp.int32(CHUNK)) + 1) * CHUNK, n_valid)
    vis_half = lambda hf: _visible(tq, half, hf * half, q_pos0, n_valid)

    @pl.when(b == 0)
    def _():
        for bb in tiles:
            qi = qi0_ref[bb]
            wi = wi0_ref[bb]
            for hf in range(2):
                kib = kib0_ref[bb, hf * half:(hf + 1) * half, :]
                acc = jnp.zeros((tq, half), _F32)
                for h in range(N_IDX_HEADS):
                    l = _dot_nt(qi[:, h * IDX_DIM:(h + 1) * IDX_DIM], kib)
                    acc = acc + jnp.maximum(l, 0.0) * wi[:, h:h + 1]
                score_s[bb, 0, hf] = jnp.where(vis_half(hf), acc, -jnp.inf)

    for bb in tiles:
        qi1 = qi1_ref[bb]
        wi1 = wi1_ref[bb]
        for h in range(N_IDX_HEADS):
            qi_s[bb, h] = qi1[:, h * IDX_DIM:(h + 1) * IDX_DIM]
            w_s[bb, h] = jnp.broadcast_to(wi1[:, h:h + 1], (tq, LANES))
    acc_s[...] = jnp.zeros((nb, 2, tq, half), _F32)

    int_min = jnp.int32(-2 ** 31)

    def decode(u):
        sk = u ^ int_min
        bits = jnp.where(sk < 0, sk ^ jnp.int32(0x7FFFFFFF), sk)
        return lax.bitcast_convert_type(bits, _F32)

    def count(bb, pred):
        return (jnp.sum(pred(score_s[bb, slot, 0]).astype(_F32), axis=-1, keepdims=True)
                + jnp.sum(pred(score_s[bb, slot, 1]).astype(_F32), axis=-1, keepdims=True))

    def step(i, us):
        h = lax.div(i, jnp.int32(2))
        hf = lax.rem(i, jnp.int32(2))
        out = []
        for bb, u in zip(tiles, us):
            cand = u | lax.shift_left(jnp.int32(1), 31 - i)
            t = decode(cand)
            out.append(jnp.where(count(bb, lambda s: s >= t) >= topk, cand, u))
            l = _dot_nt(qi_s[bb, h], kib1_ref[bb, pl.ds(pl.multiple_of(hf * half, LANES), half), :])
            wt = w_s[bb, h]
            term = jnp.concatenate([jnp.maximum(l[:, j * LANES:(j + 1) * LANES], 0.0) * wt for j in range(nblk)],
                                   axis=1)
            acc_s[bb, hf] = acc_s[bb, hf] + term
        return tuple(out)

    us = lax.fori_loop(0, n_slices, step, tuple(jnp.zeros((tq, 1), jnp.int32) for _ in tiles))
    thrs, flags = [], []
    for bb, u in zip(tiles, us):
        for hf in range(2):
            score_s[bb, nslot, hf] = jnp.where(vis_half(hf), acc_s[bb, hf], -jnp.inf)
        thr = jnp.where(n_vis <= topk, -jnp.inf, decode(u))
        n_ge = count(bb, lambda s: s >= thr)
        for hf in range(2):
            keep = (score_s[bb, slot, hf] >= thr) & vis_half(hf)
            bias_s[bb, :, hf * half:(hf + 1) * half] = jnp.where(keep, 0.0, NEG_BIG)
        thrs.append(thr)
        flags.append(jnp.max(jnp.where((n_ge > topk) & (n_vis > topk), 1.0, 0.0)))

    @pl.when(functools.reduce(jnp.maximum, flags) > 0.0)
    def _():
        earlier = (lax.broadcasted_iota(jnp.int32, (LANES, LANES), 0)
                   < lax.broadcasted_iota(jnp.int32, (LANES, LANES), 1)).astype(_BF16)
        for bb, thr in zip(tiles, thrs):
            need = topk - count(bb, lambda s: s > thr)
            run = jnp.zeros((tq, 1), _F32)
            for hf in range(2):
                for j in range(nblk):
                    c0 = hf * half + j * LANES
                    sc = score_s[bb, slot, hf, :, j * LANES:(j + 1) * LANES]
                    vis = _visible(tq, LANES, c0, q_pos0, n_valid)
                    eq = (sc == thr) & vis
                    e = eq.astype(_F32)
                    rank = _dot(e.astype(_BF16), earlier) + run
                    pick = ((sc > thr) & vis) | (eq & (rank < need))
                    bias_s[bb, :, c0:c0 + LANES] = jnp.where(pick, 0.0, NEG_BIG)
                    run = run + jnp.sum(e, axis=-1, keepdims=True)

    for bb in tiles:
        _softmax_attention(q_ref.at[bb], gate_ref.at[bb], kb_ref.at[bb], vb_ref.at[bb], bias_s.at[bb], o_ref.at[bb],
                           tq=tq, sub=sub, s_len=s_len)


def _attention_tile(obuf, q, qi, wi, gates, kb, vb, kib, *, tq, qblk, s_len, q_pos0, n_valid, topk):
    b, t, _ = q.shape
    sub = min(tq, ATTN_SUB)
    params = pltpu.CompilerParams(dimension_semantics=("arbitrary",), vmem_limit_bytes=VMEM_LIMIT)
    cur = lambda w: pl.BlockSpec((None, tq, w), lambda i: (i, qblk, 0))
    kcur = lambda w: pl.BlockSpec((None, s_len, w), lambda i: (i, 0, 0))
    out_shape = jax.ShapeDtypeStruct((b, t, ATT_WIDTH), _BF16)
    if s_len <= topk:
        assert obuf is None
        kern = functools.partial(_attn_dense_kernel, tq=tq, sub=sub, s_len=s_len, q_pos0=q_pos0, n_valid=n_valid)
        return pl.pallas_call(
            kern, grid=(b,), in_specs=[cur(ATT_WIDTH), cur(ATT_WIDTH), kcur(KV_WIDTH), kcur(KV_WIDTH)],
            out_specs=pl.BlockSpec((None, t, ATT_WIDTH), lambda i: (i, 0, 0)), out_shape=out_shape,
            scratch_shapes=[pltpu.VMEM((tq, s_len), _F32)], compiler_params=params, name="attention_dense",
        )(q, gates, kb, vb)
    if obuf is None:
        obuf = jnp.zeros(out_shape.shape, _BF16)
    nb = 1
    while (nb < ATTN_MAX_CHAINS and b % (2 * nb) == 0 and 2 * nb * tq <= ATTN_STEP_ROWS
           and 2 * nb * tq * s_len <= ATTN_STEP_ELEMS):
        nb *= 2
    steps = b // nb
    ahead = lambda i: jnp.minimum(i + 1, steps - 1)
    cur = lambda w: pl.BlockSpec((nb, tq, w), lambda i: (i, qblk, 0))
    kcur = lambda w: pl.BlockSpec((nb, s_len, w), lambda i: (i, 0, 0))
    nxt = lambda w: pl.BlockSpec((nb, tq, w), lambda i: (ahead(i), qblk, 0))
    knxt = lambda w: pl.BlockSpec((nb, s_len, w), lambda i: (ahead(i), 0, 0))
    kern = functools.partial(_attn_kernel, nb=nb, tq=tq, sub=sub, s_len=s_len, q_pos0=q_pos0, n_valid=n_valid,
                             topk=topk)
    half = s_len // 2
    return pl.pallas_call(
        kern, grid=(steps,),
        in_specs=[cur(ATT_WIDTH), cur(ATT_WIDTH), kcur(KV_WIDTH), kcur(KV_WIDTH),
                  cur(IDX_WIDTH), cur(N_IDX_HEADS), kcur(IDX_DIM),
                  nxt(IDX_WIDTH), nxt(N_IDX_HEADS), knxt(IDX_DIM),
                  pl.BlockSpec(memory_space=pl.ANY)],
        out_specs=cur(ATT_WIDTH), out_shape=out_shape, input_output_aliases={10: 0},
        scratch_shapes=[pltpu.VMEM((nb, 2, 2, tq, half), _F32), pltpu.VMEM((nb, 2, tq, half), _F32),
                        pltpu.VMEM((nb, tq, s_len), _F32),
                        pltpu.VMEM((nb, N_IDX_HEADS, tq, IDX_DIM), _BF16),
                        pltpu.VMEM((nb, N_IDX_HEADS, tq, LANES), _F32)],
        compiler_params=params, name="attention",
    )(q, gates, kb, vb, qi, wi, kib, qi, wi, kib, obuf)


def _attention(q, qi, wi, gates, kb, vb, kib, *, q_pos0, n_valid, topk, causal_tiles):
    b, t, _ = q.shape
    if causal_tiles:
        tq = _row_tile(t, ATTN_TQ)
        assert q_pos0 == 0 and tq % (2 * LANES) == 0
        obuf = None
        for j in range(t // tq):
            obuf = _attention_tile(obuf, q, qi, wi, gates, kb, vb, kib, tq=tq, qblk=j, s_len=(j + 1) * tq,
                                   q_pos0=j * tq, n_valid=n_valid, topk=topk)
        return obuf
    return _attention_tile(None, q, qi, wi, gates, kb, vb, kib, tq=t, qblk=0, s_len=kb.shape[1],
                           q_pos0=q_pos0, n_valid=n_valid, topk=topk)


def _mix_kernel(u_ref, uprev_ref, ust_ref, pin_ref, pprev_ref, pst_ref, gconv_ref, gpool_ref,
                cw_ref, cb_ref, lng_ref, lnb_ref, pw_ref, poolw_ref, pscale_ref,
                oc_ref, op_ref, xp_s, pp_s, sh_s, *, tt, q_pos0):
    t = pl.program_id(1)

    @pl.when(t == 0)
    def _():
        xp_s[0:CONV_HALO, :] = ust_ref[...]
        pp_s[0:POOL_HALO, :] = pst_ref[...]

    @pl.when(t > 0)
    def _():
        xp_s[0:CONV_HALO, :] = uprev_ref[...]
        pp_s[0:POOL_HALO, :] = pprev_ref[...]

    pos = q_pos0 + t * tt + lax.broadcasted_iota(jnp.int32, (tt, 1), 0)
    oc, op = _mix_compute(u_ref[...], pin_ref[...], gconv_ref[...].astype(_F32), gpool_ref[...].astype(_F32), pos,
                          cw_ref, cb_ref, lng_ref, lnb_ref, pw_ref, poolw_ref, pscale_ref, xp_s, pp_s, sh_s)
    oc_ref[...] = oc
    op_ref[...] = op


def _mix_compute(u, pin, gconv, gpool, pos, cw_ref, cb_ref, lng_ref, lnb_ref, pw_ref, poolw_ref, pscale_ref,
                 xp_s, pp_s, sh_s):
    tt = u.shape[0]
    xp_s[CONV_HALO:CONV_HALO + tt, :] = u
    pp_s[POOL_HALO:POOL_HALO + tt, :] = pin

    first = CONV_HALO - (CONV_K - 1)
    span = tt + CONV_HALO - SUBLANES
    acc = jnp.zeros((tt, CONV_WIDTH), _F32)
    for r in range(SUBLANES):
        if r:
            sh_s[r - 1] = xp_s[pl.ds(r, span), :]
        for j in range(CONV_K):
            if (first + j) % SUBLANES == r:
                base = first + j - r
                rows = sh_s[r - 1, pl.ds(base, tt), :] if r else xp_s[pl.ds(base, tt), :]
                acc = acc + rows * cw_ref[j:j + 1, :]
    y = acc + cb_ref[...]
    mu = jnp.mean(y, axis=-1, keepdims=True)
    yc = y - mu
    var = jnp.mean(yc * yc, axis=-1, keepdims=True)
    yn = yc * lax.rsqrt(var + EPS) * lng_ref[...] + lnb_ref[...]
    act = (yn * jax.nn.sigmoid(yn)).astype(_BF16)
    oc = (_dot(act, pw_ref[...]) * gconv).astype(_BF16)

    ops = []
    for gi, w in enumerate(POOL_WINDOWS):
        cols = slice(gi * POOL_GROUP, (gi + 1) * POOL_GROUP)
        win = pp_s[pl.ds(POOL_HALO, tt), cols]
        for i in range(1, w):
            win = win + pp_s[pl.ds(POOL_HALO - i, tt), cols]
        cnt = jnp.minimum(pos + 1, w).astype(_F32)
        r = (win / cnt - pin[:, cols]).astype(_BF16)
        yp = _dot(r, poolw_ref[gi]) * pscale_ref[:, cols]
        ops.append((yp * gpool[:, cols]).astype(_BF16))
    return oc, jnp.concatenate(ops, axis=1)


def _mixers(u, ust, pin, pst, gates, cw, cb, lng, lnb, pw, poolw, pscale, *, tt, q_pos0):
    b, t, _ = u.shape
    grid = (b, t // tt)
    cur = lambda w: pl.BlockSpec((None, tt, w), lambda i, j: (i, j, 0))

    def prev(rows, src_rows):
        if src_rows < rows:
            return None
        per = tt // rows
        return pl.BlockSpec((None, rows, CONV_WIDTH), lambda i, j: (i, jnp.maximum(j * per - 1, 0), 0))

    uprev_spec = prev(CONV_HALO, t)
    pprev_spec = prev(POOL_HALO, t)
    state = lambda rows: pl.BlockSpec((None, rows, CONV_WIDTH), lambda i, j: (i, 0, 0))
    uprev_src, pprev_src = u, pin
    if uprev_spec is None:
        assert t == tt
        uprev_src, uprev_spec = ust, state(CONV_HALO)
    if pprev_spec is None:
        assert t == tt
        pprev_src, pprev_spec = pst, state(POOL_HALO)
    gate_col = lambda blk: pl.BlockSpec((None, tt, CONV_WIDTH), lambda i, j: (i, j, blk))
    kern = functools.partial(_mix_kernel, tt=tt, q_pos0=q_pos0)
    return pl.pallas_call(
        kern, grid=grid,
        in_specs=[cur(CONV_WIDTH), uprev_spec, state(CONV_HALO), cur(POOL_WIDTH), pprev_spec, state(POOL_HALO),
                  gate_col(ATT_WIDTH // CONV_WIDTH), gate_col(ATT_WIDTH // CONV_WIDTH + 1),
                  _const_spec(cw.shape), _const_spec(cb.shape), _const_spec(lng.shape), _const_spec(lnb.shape),
                  _const_spec(pw.shape), _const_spec(poolw.shape), _const_spec(pscale.shape)],
        out_specs=[cur(CONV_WIDTH), cur(POOL_WIDTH)],
        out_shape=[jax.ShapeDtypeStruct((b, t, CONV_WIDTH), _BF16), jax.ShapeDtypeStruct((b, t, POOL_WIDTH), _BF16)],
        scratch_shapes=[pltpu.VMEM((CONV_HALO + tt, CONV_WIDTH), _F32), pltpu.VMEM((POOL_HALO + tt, POOL_WIDTH), _F32),
                        pltpu.VMEM((SUBLANES - 1, tt + CONV_HALO - SUBLANES, CONV_WIDTH), _F32)],
        compiler_params=pltpu.CompilerParams(dimension_semantics=("arbitrary", "arbitrary"),
                                             vmem_limit_bytes=VMEM_LIMIT),
        name="mixers",
    )(u, uprev_src, ust, pin, pprev_src, pst, gates, gates, cw, cb, lng, lnb, pw, poolw, pscale)


def _outproj_kernel(x_ref, a_ref, c_ref, pl_ref, p_ref, woa_ref, woc_ref, wop_ref, nple_ref,
                    wgate_ref, wproj_ref, nfin_ref, o_ref, *, final):
    x1 = (x_ref[...] + _dot(a_ref[...], woa_ref[...]) + _dot(c_ref[...], woc_ref[...])
          + _dot(pl_ref[...], wop_ref[...]))
    xn = _rms(x1, nple_ref[...]).astype(_BF16)
    gate = jax.nn.sigmoid(_dot(xn, wgate_ref[...]))
    x2 = x1 + gate * _dot(p_ref[...].astype(_BF16), wproj_ref[...])
    o_ref[...] = _rms(x2, nfin_ref[...]) if final else x2


def _outproj(x2, a, c, po, p_all, layer, w_out, nple, wgate, wproj, nfin, *, tm, final):
    n = x2.shape[0]
    row = lambda w: pl.BlockSpec((tm, w), lambda i: (i, 0))
    wblk = lambda rows, blk: pl.BlockSpec((rows, D_MODEL), lambda i: (blk, 0), pipeline_mode=pl.Buffered(1))
    p_spec = pl.BlockSpec((None, tm, PLE_DIM), lambda i: (layer, i, 0))
    return pl.pallas_call(
        functools.partial(_outproj_kernel, final=final), grid=(n // tm,),
        in_specs=[row(D_MODEL), row(ATT_WIDTH), row(CONV_WIDTH), row(POOL_WIDTH), p_spec,
                  wblk(ATT_WIDTH, 0), wblk(CONV_WIDTH, ATT_WIDTH // CONV_WIDTH),
                  wblk(POOL_WIDTH, ATT_WIDTH // CONV_WIDTH + 1),
                  _const_spec(nple.shape), _const_spec(wgate.shape), _const_spec(wproj.shape),
                  _const_spec(nfin.shape)],
        out_specs=row(D_MODEL),
        out_shape=jax.ShapeDtypeStruct((n, D_MODEL), _F32),
        compiler_params=pltpu.CompilerParams(dimension_semantics=("arbitrary",),
                                             vmem_limit_bytes=VMEM_LIMIT),
        name="outproj",
    )(x2, a, c, po, p_all, w_out, w_out, w_out, nple, wgate, wproj, nfin)


def _rope_tables(pos):
    pos = pos.astype(_F32)[:, None]

    def cs(half):
        freq = ROPE_THETA ** (-jnp.arange(half, dtype=_F32) / half)
        ang = pos * freq[None, :]
        return jnp.cos(ang), jnp.sin(ang)

    c, s = cs(HEAD_DIM // 2)
    c128 = jnp.concatenate([c, c], axis=-1)
    s128 = jnp.concatenate([-s, s], axis=-1)
    c, s = cs(IDX_DIM // 2)
    z = jnp.zeros_like(s)
    c64 = jnp.concatenate([c, c, c, c], axis=-1)
    s64a = jnp.concatenate([-s, z, -s, z], axis=-1)
    s64b = jnp.concatenate([z, s, z, s], axis=-1)
    return c128, s128, c64, s64a, s64b


def _split_w_in(w):
    w = w.astype(_BF16)
    cols = lambda a, n: w[:, a:a + n]
    pad = jnp.zeros((w.shape[0], LANES - IDX_DIM - N_IDX_HEADS), _BF16)
    wqk = jnp.concatenate([cols(_OFF_Q, ATT_WIDTH), cols(_OFF_K, KV_WIDTH)], axis=1)
    wv = cols(_OFF_V, KV_WIDTH)
    widx = jnp.concatenate([cols(_OFF_QI, IDX_WIDTH), cols(_OFF_KI, IDX_DIM), cols(_OFF_WI, N_IDX_HEADS), pad], axis=1)
    wgate = jnp.concatenate([cols(_OFF_GATT, ATT_WIDTH), cols(_OFF_GCONV, CONV_WIDTH), cols(_OFF_GPOOL, POOL_WIDTH)], axis=1)
    wglu = cols(_OFF_GLU, 2 * CONV_WIDTH)
    wpool = cols(_OFF_POOL, POOL_WIDTH)
    return wqk, wv, widx, wgate, wglu, wpool


def _row_tile(n, pref):
    t = min(pref, n)
    while n % t:
        t //= 2
    return t


def _layer(x, p_all, kv_prev, layer, pos0, k_past, v_past, ki_past, conv_prev, pool_prev, topk, lw, nfin, final):
    b, t, _ = x.shape
    n = b * t
    (norm_mix, ws, cw, cb, lng, lnb, pw, poolw, pscale, w_out, nple, wgate, wproj) = lw

    tm = _row_tile(t, 256) if b > 1 and t >= 256 else n
    if tm <= t:
        tabs = _rope_tables(pos0 + jnp.arange(t))
        tiles_per_seq = t // tm
    else:
        tabs = tuple(jnp.tile(a, (b, 1)) for a in _rope_tables(pos0 + jnp.arange(t)))
        tiles_per_seq = 1
    ust = jnp.pad(conv_prev, ((0, 0), (CONV_HALO - (CONV_K - 1), 0), (0, 0)))
    pst = jnp.pad(pool_prev, ((0, 0), (POOL_HALO - POOL_HIST, 0), (0, 0)))
    fused = CONV_HALO <= tm <= t
    mix_ops = (ust, pst, cw, cb, lng, lnb, pw, poolw, pscale) if fused else None
    (q, k_all, v_all, kb, vb, qi, ki, kib, wi, gates, u, pin, *mixed) = _inproj(
        x.reshape(n, D_MODEL), norm_mix, tabs, ws, kv_prev, mix_ops, layer=layer, depth=p_all.shape[0], tm=tm,
        tiles_per_seq=tiles_per_seq, q_pos0=pos0)
    r3 = lambda a: a.reshape(b, t, a.shape[-1])

    kb3, vb3, kib3 = r3(kb), r3(vb), r3(kib)
    if k_past is not None:
        past = k_past.shape[1]
        kb3 = jnp.concatenate([k_past.reshape(b, past, KV_WIDTH).astype(_BF16), kb3], axis=1)
        vb3 = jnp.concatenate([v_past.reshape(b, past, KV_WIDTH).astype(_BF16), vb3], axis=1)
        kib3 = jnp.concatenate([ki_past.astype(_BF16), kib3], axis=1)
    n_valid = kb3.shape[1]
    s_pad = -(-n_valid // (2 * LANES)) * (2 * LANES)
    if s_pad != n_valid:
        padk = lambda a: jnp.pad(a, ((0, 0), (0, s_pad - n_valid), (0, 0)))
        kb3, vb3, kib3 = padk(kb3), padk(vb3), padk(kib3)
    o_att = _attention(r3(q), r3(qi), r3(wi), r3(gates), kb3, vb3, kib3, q_pos0=pos0, n_valid=n_valid,
                       topk=topk, causal_tiles=k_past is None and t % (2 * LANES) == 0)

    if fused:
        oc, op = mixed
        u3, pin3 = u, pin
    else:
        u3, pin3 = r3(u), r3(pin)
        oc, op = _mixers(u3, ust, pin3, pst, r3(gates), cw, cb, lng, lnb, pw, poolw, pscale,
                         tt=_row_tile(t, 256), q_pos0=pos0)

    tm2 = _row_tile(n, 256)
    x_new = _outproj(x.reshape(n, D_MODEL), o_att.reshape(n, ATT_WIDTH), oc.reshape(n, CONV_WIDTH),
                     op.reshape(n, POOL_WIDTH), p_all.reshape(-1, n, PLE_DIM), layer, w_out, nple, wgate, wproj,
                     nfin, tm=tm2, final=final).reshape(b, t, D_MODEL)

    conv_state = jnp.concatenate([conv_prev, u3], axis=1)[:, -(CONV_K - 1):]
    pool_state = jnp.concatenate([pool_prev, pin3], axis=1)[:, -POOL_HIST:]
    return x_new, (k_all, v_all), r3(ki), conv_state, pool_state


def kernel(x_prompt, x_sample, p_prompt, p_sample, cache_k, cache_v, cache_kidx, state_conv, state_pool, norm_mix, w_in, conv_w, conv_b, conv_ln_g, conv_ln_b, conv_pw, pool_w, pool_scale, w_out, norm_ple, w_ple_gate, w_ple_proj, norm_final):
    bp, tp, _ = x_prompt.shape
    bs, ts, _ = x_sample.shape
    depth = w_in.shape[0]
    past = cache_k.shape[2]
    topk_p = min(TOPK_MAX, tp // 4)
    topk_s = min(TOPK_MAX, (past + ts) // 4)
    conv0 = jnp.zeros((bp, CONV_K - 1, CONV_WIDTH), x_prompt.dtype)
    pool0 = jnp.zeros((bp, POOL_HIST, POOL_WIDTH), x_prompt.dtype)
    nfin = norm_final.reshape(1, D_MODEL)
    hp, hs = x_prompt, x_sample
    kv_p = kv_s = None
    outs_p, outs_s = [], []
    for i in range(depth):
        vec = lambda a: a[i].reshape(1, -1)
        lw = (vec(norm_mix), _split_w_in(w_in[i]), conv_w[i], vec(conv_b), vec(conv_ln_g), vec(conv_ln_b),
              conv_pw[i].astype(_BF16), pool_w[i].astype(_BF16), vec(pool_scale), w_out[i].astype(_BF16),
              vec(norm_ple), w_ple_gate[i].astype(_BF16), w_ple_proj[i].astype(_BF16))
        final = i == depth - 1
        hp, kv_p, *op_ = _layer(hp, p_prompt, kv_p, i, 0, None, None, None, conv0, pool0, topk_p, lw, nfin, final)
        hs, kv_s, *os_ = _layer(hs, p_sample, kv_s, i, past, cache_k[i], cache_v[i], cache_kidx[i],
                                state_conv[i], state_pool[i], topk_s, lw, nfin, final)
        outs_p.append(op_)
        outs_s.append(os_)
    stack = lambda outs, j: jnp.stack([o[j] for o in outs])
    heads = lambda a, b, t: a.reshape(depth, b, t, N_KV_HEADS, HEAD_DIM)
    return ((hp, hs, heads(kv_p[0], bp, tp), heads(kv_p[1], bp, tp)) + tuple(stack(outs_p, j) for j in range(3))
            + (heads(kv_s[0], bs, ts), heads(kv_s[1], bs, ts)) + tuple(stack(outs_s, j) for j in range(3)))
```

```python
import functools

import numpy as np
import jax
import jax.numpy as jnp
from jax import lax
from jax.experimental import pallas as pl
from jax.experimental.pallas import tpu as pltpu

D_MODEL = 2048
CHUNK = 64
HEAD_DIM = 128
ATT_WIDTH = D_MODEL // 2
CONV_WIDTH = D_MODEL // 4
POOL_WIDTH = D_MODEL // 4
N_HEADS = ATT_WIDTH // HEAD_DIM
N_KV_HEADS = 2
HEADS_PER_KV = N_HEADS // N_KV_HEADS
KV_WIDTH = N_KV_HEADS * HEAD_DIM
N_IDX_HEADS = 16
IDX_DIM = 64
IDX_WIDTH = N_IDX_HEADS * IDX_DIM
TOPK_MAX = 256
ROPE_THETA = 10000.0
CONV_K = 31
CONV_HALO = 32
POOL_WINDOWS = (2, 4, 8, 16)
POOL_GROUP = POOL_WIDTH // len(POOL_WINDOWS)
POOL_HIST = 15
POOL_HALO = 16
PLE_DIM = 256
EPS = 1e-6
LANES = 128
SUBLANES = 8
NEG_BIG = -1e30
VMEM_LIMIT = 60 * 1024 * 1024
N_INPROJ_OUTS = 12
N_MIX_IN = 9
N_MIX_TAIL = 5
ATTN_TQ = 256
ATTN_SUB = 128
ATTN_STEP_ELEMS = 2 * 256 * 768
ATTN_STEP_ROWS = 512
ATTN_MAX_CHAINS = 8

_OFF_Q = 0
_OFF_K = _OFF_Q + ATT_WIDTH
_OFF_V = _OFF_K + KV_WIDTH
_OFF_QI = _OFF_V + KV_WIDTH
_OFF_KI = _OFF_QI + IDX_WIDTH
_OFF_WI = _OFF_KI + IDX_DIM
_OFF_GATT = _OFF_WI + N_IDX_HEADS
_OFF_GLU = _OFF_GATT + ATT_WIDTH
_OFF_GCONV = _OFF_GLU + 2 * CONV_WIDTH
_OFF_POOL = _OFF_GCONV + CONV_WIDTH
_OFF_GPOOL = _OFF_POOL + POOL_WIDTH
N_IN = _OFF_GPOOL + POOL_WIDTH

_F32 = jnp.float32
_BF16 = jnp.bfloat16


def _const_spec(shape):
    nd = len(shape)
    return pl.BlockSpec(shape, lambda *_: (0,) * nd, pipeline_mode=pl.Buffered(1))


def _dot(a, b):
    return jnp.dot(a, b, preferred_element_type=_F32)


def _dot_nt(a, b):
    return lax.dot_general(a, b, (((1,), (1,)), ((), ())), preferred_element_type=_F32)


def _rms(x, g):
    return x * lax.rsqrt(jnp.mean(x * x, axis=-1, keepdims=True) + EPS) * g


def _inproj_kernel(x_ref, g_ref, c128_ref, s128_ref, c64_ref, s64a_ref, s64b_ref,
                   wqk_ref, wv_ref, widx_ref, wgate_ref, wglu_ref, wpool_ref, *rest, first_plane, mix):
    tm = x_ref.shape[0]
    if mix is not None:
        (ust_ref, pst_ref, cw_ref, cb_ref, lng_ref, lnb_ref, pw_ref, poolw_ref, pscale_ref) = rest[:N_MIX_IN]
        oc_ref, op_ref, xp_s, pp_s, sh_s = rest[-N_MIX_TAIL:]
        rest = rest[N_MIX_IN:-N_MIX_TAIL]
        t = lax.rem(pl.program_id(0), mix[1])

        @pl.when(t == 0)
        def _():
            xp_s[0:CONV_HALO, :] = ust_ref[...]
            pp_s[0:POOL_HALO, :] = pst_ref[...]

        @pl.when(t > 0)
        def _():
            xp_s[0:CONV_HALO, :] = xp_s[tm:tm + CONV_HALO, :]
            pp_s[0:POOL_HALO, :] = pp_s[tm:tm + POOL_HALO, :]

    (q_ref, k_ref, v_ref, kb_ref, vb_ref, qi_ref, ki_ref, kib_ref, wi_ref,
     gates_ref, u_ref, pin_ref) = rest[-N_INPROJ_OUTS:]
    if first_plane is not None:
        for ref in (k_ref, v_ref):
            for d in range(ref.shape[0]):
                if d != first_plane:
                    ref[d] = jnp.zeros(ref.shape[1:], _F32)
        k_ref, v_ref = k_ref.at[first_plane], v_ref.at[first_plane]
    xn = _rms(x_ref[...], g_ref[...]).astype(_BF16)

    zg = _dot(xn, wgate_ref[...])
    gates = zg * jax.nn.sigmoid(zg)
    gates_ref[...] = gates.astype(_BF16)

    zu = _dot(xn, wglu_ref[...])
    u = zu[:, :CONV_WIDTH] * jax.nn.sigmoid(zu[:, CONV_WIDTH:])
    pin = _dot(xn, wpool_ref[...])
    if mix is None:
        u_ref[...] = u
        pin_ref[...] = pin
    else:
        pos = mix[0] + t * tm + lax.broadcasted_iota(jnp.int32, (tm, 1), 0)
        oc, op = _mix_compute(u, pin, gates[:, ATT_WIDTH:ATT_WIDTH + CONV_WIDTH], gates[:, ATT_WIDTH + CONV_WIDTH:],
                              pos, cw_ref, cb_ref, lng_ref, lnb_ref, pw_ref, poolw_ref, pscale_ref, xp_s, pp_s, sh_s)
        oc_ref[...] = oc
        op_ref[...] = op
        u_ref[...] = u[tm - CONV_HALO:, :]
        pin_ref[...] = pin[tm - POOL_HALO:, :]

    zqk = _dot(xn, wqk_ref[...])
    c128 = c128_ref[...]
    s128 = s128_ref[...]
    for h in range(N_HEADS + N_KV_HEADS):
        xh = zqk[:, h * HEAD_DIM:(h + 1) * HEAD_DIM]
        r = xh * c128 + pltpu.roll(xh, HEAD_DIM // 2, 1) * s128
        if h < N_HEADS:
            q_ref[:, h * HEAD_DIM:(h + 1) * HEAD_DIM] = (r * (HEAD_DIM ** -0.5)).astype(_BF16)
        else:
            j = h - N_HEADS
            k_ref[pl.ds(j, tm, stride=N_KV_HEADS), :] = r
            kb_ref[:, j * HEAD_DIM:(j + 1) * HEAD_DIM] = r.astype(_BF16)

    zv = _dot(xn, wv_ref[...])
    for j in range(N_KV_HEADS):
        v_ref[pl.ds(j, tm, stride=N_KV_HEADS), :] = zv[:, j * HEAD_DIM:(j + 1) * HEAD_DIM]
    vb_ref[...] = zv.astype(_BF16)

    zi = _dot(xn, widx_ref[...])
    c64 = c64_ref[...]
    s64a = s64a_ref[...]
    s64b = s64b_ref[...]
    n_groups = IDX_WIDTH // LANES
    for j in range(n_groups + 1):
        xj = zi[:, j * LANES:(j + 1) * LANES]
        r = (xj * c64 + pltpu.roll(xj, LANES - IDX_DIM // 2, 1) * s64a
             + pltpu.roll(xj, IDX_DIM // 2, 1) * s64b)
        if j < n_groups:
            qi_ref[:, j * LANES:(j + 1) * LANES] = r.astype(_BF16)
        else:
            ki_ref[...] = r[:, :IDX_DIM]
            kib_ref[...] = r[:, :IDX_DIM].astype(_BF16)
            wi_ref[...] = xj[:, IDX_DIM:IDX_DIM + N_IDX_HEADS] * (IDX_DIM ** -0.5 * N_IDX_HEADS ** -0.5)


def _inproj(x2, norm, tabs, ws, kv_prev, mix_ops, *, layer, depth, tm, tiles_per_seq, q_pos0):
    n = x2.shape[0]
    grid = (n // tm,)
    row = lambda w: pl.BlockSpec((tm, w), lambda i: (i, 0))
    tab = pl.BlockSpec((tm, LANES), lambda i: (i % tiles_per_seq, 0))
    kv_shape = jax.ShapeDtypeStruct((depth, n * N_KV_HEADS, HEAD_DIM), _F32)
    if kv_prev is None:
        kv_spec = pl.BlockSpec((depth, tm * N_KV_HEADS, HEAD_DIM), lambda i: (0, i, 0))
    else:
        kv_spec = pl.BlockSpec((None, tm * N_KV_HEADS, HEAD_DIM), lambda i: (layer, i, 0))
    out_shape = [
        jax.ShapeDtypeStruct((n, ATT_WIDTH), _BF16),
        kv_shape,
        kv_shape,
        jax.ShapeDtypeStruct((n, KV_WIDTH), _BF16),
        jax.ShapeDtypeStruct((n, KV_WIDTH), _BF16),
        jax.ShapeDtypeStruct((n, IDX_WIDTH), _BF16),
        jax.ShapeDtypeStruct((n, IDX_DIM), _F32),
        jax.ShapeDtypeStruct((n, IDX_DIM), _BF16),
        jax.ShapeDtypeStruct((n, N_IDX_HEADS), _F32),
        jax.ShapeDtypeStruct((n, D_MODEL), _BF16),
        jax.ShapeDtypeStruct((n, CONV_WIDTH), _F32),
        jax.ShapeDtypeStruct((n, POOL_WIDTH), _F32),
    ]
    assert len(out_shape) == N_INPROJ_OUTS
    out_specs = [kv_spec if s is kv_shape else row(s.shape[1]) for s in out_shape]
    in_specs = ([row(D_MODEL), _const_spec((1, D_MODEL))] + [tab] * 5
                + [_const_spec(w.shape) for w in ws])
    args = (x2, norm, *tabs, *ws)
    scratch = []
    if mix_ops is not None:
        assert len(mix_ops) == N_MIX_IN and tm >= CONV_HALO
        n_seq = n // (tm * tiles_per_seq)
        seq = lambda rows: pl.BlockSpec((None, rows, CONV_WIDTH), lambda i: (i // tiles_per_seq, 0, 0))
        in_specs += [seq(CONV_HALO), seq(POOL_HALO)] + [_const_spec(w.shape) for w in mix_ops[2:]]
        args += tuple(mix_ops)
        out_shape[-2:] = [jax.ShapeDtypeStruct((n_seq, CONV_HALO, CONV_WIDTH), _F32),
                          jax.ShapeDtypeStruct((n_seq, POOL_HALO, POOL_WIDTH), _F32)]
        out_specs[-2:] = [seq(CONV_HALO), seq(POOL_HALO)]
        out_shape += [jax.ShapeDtypeStruct((n, CONV_WIDTH), _BF16), jax.ShapeDtypeStruct((n, POOL_WIDTH), _BF16)]
        out_specs += [row(CONV_WIDTH), row(POOL_WIDTH)]
        scratch = [pltpu.VMEM((CONV_HALO + tm, CONV_WIDTH), _F32), pltpu.VMEM((POOL_HALO + tm, POOL_WIDTH), _F32),
                   pltpu.VMEM((SUBLANES - 1, tm + CONV_HALO - SUBLANES, CONV_WIDTH), _F32)]
        assert len(scratch) + 2 == N_MIX_TAIL
    aliases = {}
    if kv_prev is not None:
        aliases = {len(args): 1, len(args) + 1: 2}
        in_specs = in_specs + [pl.BlockSpec(memory_space=pl.ANY)] * 2
        args = args + tuple(kv_prev)
    return pl.pallas_call(
        functools.partial(_inproj_kernel, first_plane=layer if kv_prev is None else None,
                          mix=None if mix_ops is None else (q_pos0, tiles_per_seq)),
        grid=grid, in_specs=in_specs, out_specs=out_specs, out_shape=out_shape, scratch_shapes=scratch,
        input_output_aliases=aliases,
        compiler_params=pltpu.CompilerParams(dimension_semantics=("arbitrary",),
                                             vmem_limit_bytes=VMEM_LIMIT),
        name="inproj",
    )(*args)


def _softmax_attention(q_ref, gate_ref, kb_ref, vb_ref, bias_s, o_ref, *, tq, sub, s_len):
    for r0 in range(0, tq, sub):
        rows = slice(r0, r0 + sub)
        q = q_ref[rows, :]
        gate = gate_ref[rows, :].astype(_F32)
        bias = bias_s[rows, :]
        for g in range(N_KV_HEADS):
            heads = [g * HEADS_PER_KV + r for r in range(HEADS_PER_KV)]
            qg = jnp.concatenate([q[:, h * HEAD_DIM:(h + 1) * HEAD_DIM] for h in heads], axis=0)
            s = _dot_nt(qg, kb_ref[:, g * HEAD_DIM:(g + 1) * HEAD_DIM])
            s = (s.reshape(HEADS_PER_KV, sub, s_len) + bias[None]).reshape(HEADS_PER_KV * sub, s_len)
            m = jnp.max(s, axis=-1, keepdims=True)
            p = jnp.exp(s - m)
            den = jnp.sum(p, axis=-1, keepdims=True)
            o = _dot(p.astype(_BF16), vb_ref[:, g * HEAD_DIM:(g + 1) * HEAD_DIM]) / den
            for r, h in enumerate(heads):
                cols = slice(h * HEAD_DIM, (h + 1) * HEAD_DIM)
                o_ref[rows, cols] = (o[r * sub:(r + 1) * sub] * gate[:, cols]).astype(_BF16)


def _visible(tq, cols, col0, q_pos0, n_valid):
    qpos = q_pos0 + lax.broadcasted_iota(jnp.int32, (tq, 1), 0)
    kpos = col0 + lax.broadcasted_iota(jnp.int32, (1, cols), 1)
    return (lax.div(kpos, jnp.int32(CHUNK)) <= lax.div(qpos, jnp.int32(CHUNK))) & (kpos < n_valid)


def _attn_dense_kernel(q_ref, gate_ref, kb_ref, vb_ref, o_ref, bias_s, *, tq, sub, s_len, q_pos0, n_valid):
    bias_s[...] = jnp.where(_visible(tq, s_len, 0, q_pos0, n_valid), 0.0, NEG_BIG)
    if o_ref.shape[0] > tq:
        o_ref[tq:, :] = jnp.zeros((o_ref.shape[0] - tq, o_ref.shape[1]), o_ref.dtype)
    _softmax_attention(q_ref, gate_ref, kb_ref, vb_ref, bias_s, o_ref, tq=tq, sub=sub, s_len=s_len)


def _attn_kernel(q_ref, gate_ref, kb_ref, vb_ref, qi0_ref, wi0_ref, kib0_ref, qi1_ref, wi1_ref, kib1_ref,
                 obuf_ref, o_ref, score_s, acc_s, bias_s, qi_s, w_s, *, nb, tq, sub, s_len, q_pos0, n_valid, topk):
    del obuf_ref
    b = pl.program_id(0)
    slot = lax.rem(b, 2)
    nslot = 1 - slot
    half = s_len // 2
    nblk = half // LANES
    n_slices = 2 * N_IDX_HEADS
    assert n_slices == 32
    tiles = range(nb)

    qpos = q_pos0 + lax.broadcasted_iota(jnp.int32, (tq, 1), 0)
    n_vis = jnp.minimum((lax.div(qpos, jnp.int32(CHUNK)) + 1) * CHUNK, n_valid)
    vis_half = lambda hf: _visible(tq, half, hf * half, q_pos0, n_valid)

    @pl.when(b == 0)
    def _():
        for bb in tiles:
            qi = qi0_ref[bb]
            wi = wi0_ref[bb]
            for hf in range(2):
                kib = kib0_ref[bb, hf * half:(hf + 1) * half, :]
                acc = jnp.zeros((tq, half), _F32)
                for h in range(N_IDX_HEADS):
                    l = _dot_nt(qi[:, h * IDX_DIM:(h + 1) * IDX_DIM], kib)
                    acc = acc + jnp.maximum(l, 0.0) * wi[:, h:h + 1]
                score_s[bb, 0, hf] = jnp.where(vis_half(hf), acc, -jnp.inf)

    for bb in tiles:
        qi1 = qi1_ref[bb]
        wi1 = wi1_ref[bb]
        for h in range(N_IDX_HEADS):
            qi_s[bb, h] = qi1[:, h * IDX_DIM:(h + 1) * IDX_DIM]
            w_s[bb, h] = jnp.broadcast_to(wi1[:, h:h + 1], (tq, LANES))
    acc_s[...] = jnp.zeros((nb, 2, tq, half), _F32)

    int_min = jnp.int32(-2 ** 31)

    def decode(u):
        sk = u ^ int_min
        bits = jnp.where(sk < 0, sk ^ jnp.int32(0x7FFFFFFF), sk)
        return lax.bitcast_convert_type(bits, _F32)

    def count(bb, pred):
        return (jnp.sum(pred(score_s[bb, slot, 0]).astype(_F32), axis=-1, keepdims=True)
                + jnp.sum(pred(score_s[bb, slot, 1]).astype(_F32), axis=-1, keepdims=True))

    def step(i, us):
        h = lax.div(i, jnp.int32(2))
        hf = lax.rem(i, jnp.int32(2))
        out = []
        for bb, u in zip(tiles, us):
            cand = u | lax.shift_left(jnp.int32(1), 31 - i)
            t = decode(cand)
            out.append(jnp.where(count(bb, lambda s: s >= t) >= topk, cand, u))
            l = _dot_nt(qi_s[bb, h], kib1_ref[bb, pl.ds(pl.multiple_of(hf * half, LANES), half), :])
            wt = w_s[bb, h]
            term = jnp.concatenate([jnp.maximum(l[:, j * LANES:(j + 1) * LANES], 0.0) * wt for j in range(nblk)],
                                   axis=1)
            acc_s[bb, hf] = acc_s[bb, hf] + term
        return tuple(out)

    us = lax.fori_loop(0, n_slices, step, tuple(jnp.zeros((tq, 1), jnp.int32) for _ in tiles))
    thrs, flags = [], []
    for bb, u in zip(tiles, us):
        for hf in range(2):
            score_s[bb, nslot, hf] = jnp.where(vis_half(hf), acc_s[bb, hf], -jnp.inf)
        thr = jnp.where(n_vis <= topk, -jnp.inf, decode(u))
        n_ge = count(bb, lambda s: s >= thr)
        for hf in range(2):
            keep = (score_s[bb, slot, hf] >= thr) & vis_half(hf)
            bias_s[bb, :, hf * half:(hf + 1) * half] = jnp.where(keep, 0.0, NEG_BIG)
        thrs.append(thr)
        flags.append(jnp.max(jnp.where((n_ge > topk) & (n_vis > topk), 1.0, 0.0)))

    @pl.when(functools.reduce(jnp.maximum, flags) > 0.0)
    def _():
        earlier = (lax.broadcasted_iota(jnp.int32, (LANES, LANES), 0)
                   < lax.broadcasted_iota(jnp.int32, (LANES, LANES), 1)).astype(_BF16)
        for bb, thr in zip(tiles, thrs):
            need = topk - count(bb, lambda s: s > thr)
            run = jnp.zeros((tq, 1), _F32)
            for hf in range(2):
                for j in range(nblk):
                    c0 = hf * half + j * LANES
                    sc = score_s[bb, slot, hf, :, j * LANES:(j + 1) * LANES]
                    vis = _visible(tq, LANES, c0, q_pos0, n_valid)
                    eq = (sc == thr) & vis
                    e = eq.astype(_F32)
                    rank = _dot(e.astype(_BF16), earlier) + run
                    pick = ((sc > thr) & vis) | (eq & (rank < need))
                    bias_s[bb, :, c0:c0 + LANES] = jnp.where(pick, 0.0, NEG_BIG)
                    run = run + jnp.sum(e, axis=-1, keepdims=True)

    for bb in tiles:
        _softmax_attention(q_ref.at[bb], gate_ref.at[bb], kb_ref.at[bb], vb_ref.at[bb], bias_s.at[bb], o_ref.at[bb],
                           tq=tq, sub=sub, s_len=s_len)


def _attention_tile(obuf, q, qi, wi, gates, kb, vb, kib, *, tq, qblk, s_len, q_pos0, n_valid, topk):
    b, t, _ = q.shape
    sub = min(tq, ATTN_SUB)
    params = pltpu.CompilerParams(dimension_semantics=("arbitrary",), vmem_limit_bytes=VMEM_LIMIT)
    cur = lambda w: pl.BlockSpec((None, tq, w), lambda i: (i, qblk, 0))
    kcur = lambda w: pl.BlockSpec((None, s_len, w), lambda i: (i, 0, 0))
    out_shape = jax.ShapeDtypeStruct((b, t, ATT_WIDTH), _BF16)
    if s_len <= topk:
        assert obuf is None
        kern = functools.partial(_attn_dense_kernel, tq=tq, sub=sub, s_len=s_len, q_pos0=q_pos0, n_valid=n_valid)
        return pl.pallas_call(
            kern, grid=(b,), in_specs=[cur(ATT_WIDTH), cur(ATT_WIDTH), kcur(KV_WIDTH), kcur(KV_WIDTH)],
            out_specs=pl.BlockSpec((None, t, ATT_WIDTH), lambda i: (i, 0, 0)), out_shape=out_shape,
            scratch_shapes=[pltpu.VMEM((tq, s_len), _F32)], compiler_params=params, name="attention_dense",
        )(q, gates, kb, vb)
    if obuf is None:
        obuf = jnp.zeros(out_shape.shape, _BF16)
    nb = 1
    while (nb < ATTN_MAX_CHAINS and b % (2 * nb) == 0 and 2 * nb * tq <= ATTN_STEP_ROWS
           and 2 * nb * tq * s_len <= ATTN_STEP_ELEMS):
        nb *= 2
    steps = b // nb
    ahead = lambda i: jnp.minimum(i + 1, steps - 1)
    cur = lambda w: pl.BlockSpec((nb, tq, w), lambda i: (i, qblk, 0))
    kcur = lambda w: pl.BlockSpec((nb, s_len, w), lambda i: (i, 0, 0))
    nxt = lambda w: pl.BlockSpec((nb, tq, w), lambda i: (ahead(i), qblk, 0))
    knxt = lambda w: pl.BlockSpec((nb, s_len, w), lambda i: (ahead(i), 0, 0))
    kern = functools.partial(_attn_kernel, nb=nb, tq=tq, sub=sub, s_len=s_len, q_pos0=q_pos0, n_valid=n_valid,
                             topk=topk)
    half = s_len // 2
    return pl.pallas_call(
        kern, grid=(steps,),
        in_specs=[cur(ATT_WIDTH), cur(ATT_WIDTH), kcur(KV_WIDTH), kcur(KV_WIDTH),
                  cur(IDX_WIDTH), cur(N_IDX_HEADS), kcur(IDX_DIM),
                  nxt(IDX_WIDTH), nxt(N_IDX_HEADS), knxt(IDX_DIM),
                  pl.BlockSpec(memory_space=pl.ANY)],
        out_specs=cur(ATT_WIDTH), out_shape=out_shape, input_output_aliases={10: 0},
        scratch_shapes=[pltpu.VMEM((nb, 2, 2, tq, half), _F32), pltpu.VMEM((nb, 2, tq, half), _F32),
                        pltpu.VMEM((nb, tq, s_len), _F32),
                        pltpu.VMEM((nb, N_IDX_HEADS, tq, IDX_DIM), _BF16),
                        pltpu.VMEM((nb, N_IDX_HEADS, tq, LANES), _F32)],
        compiler_params=params, name="attention",
    )(q, gates, kb, vb, qi, wi, kib, qi, wi, kib, obuf)


def _attention(q, qi, wi, gates, kb, vb, kib, *, q_pos0, n_valid, topk, causal_tiles):
    b, t, _ = q.shape
    if causal_tiles:
        tq = _row_tile(t, ATTN_TQ)
        assert q_pos0 == 0 and tq % (2 * LANES) == 0
        obuf = None
        for j in range(t // tq):
            obuf = _attention_tile(obuf, q, qi, wi, gates, kb, vb, kib, tq=tq, qblk=j, s_len=(j + 1) * tq,
                                   q_pos0=j * tq, n_valid=n_valid, topk=topk)
        return obuf
    return _attention_tile(None, q, qi, wi, gates, kb, vb, kib, tq=t, qblk=0, s_len=kb.shape[1],
                           q_pos0=q_pos0, n_valid=n_valid, topk=topk)


def _mix_kernel(u_ref, uprev_ref, ust_ref, pin_ref, pprev_ref, pst_ref, gconv_ref, gpool_ref,
                cw_ref, cb_ref, lng_ref, lnb_ref, pw_ref, poolw_ref, pscale_ref,
                oc_ref, op_ref, xp_s, pp_s, sh_s, *, tt, q_pos0):
    t = pl.program_id(1)

    @pl.when(t == 0)
    def _():
        xp_s[0:CONV_HALO, :] = ust_ref[...]
        pp_s[0:POOL_HALO, :] = pst_ref[...]

    @pl.when(t > 0)
    def _():
        xp_s[0:CONV_HALO, :] = uprev_ref[...]
        pp_s[0:POOL_HALO, :] = pprev_ref[...]

    pos = q_pos0 + t * tt + lax.broadcasted_iota(jnp.int32, (tt, 1), 0)
    oc, op = _mix_compute(u_ref[...], pin_ref[...], gconv_ref[...].astype(_F32), gpool_ref[...].astype(_F32), pos,
                          cw_ref, cb_ref, lng_ref, lnb_ref, pw_ref, poolw_ref, pscale_ref, xp_s, pp_s, sh_s)
    oc_ref[...] = oc
    op_ref[...] = op


def _mix_compute(u, pin, gconv, gpool, pos, cw_ref, cb_ref, lng_ref, lnb_ref, pw_ref, poolw_ref, pscale_ref,
                 xp_s, pp_s, sh_s):
    tt = u.shape[0]
    xp_s[CONV_HALO:CONV_HALO + tt, :] = u
    pp_s[POOL_HALO:POOL_HALO + tt, :] = pin

    first = CONV_HALO - (CONV_K - 1)
    span = tt + CONV_HALO - SUBLANES
    acc = jnp.zeros((tt, CONV_WIDTH), _F32)
    for r in range(SUBLANES):
        if r:
            sh_s[r - 1] = xp_s[pl.ds(r, span), :]
        for j in range(CONV_K):
            if (first + j) % SUBLANES == r:
                base = first + j - r
                rows = sh_s[r - 1, pl.ds(base, tt), :] if r else xp_s[pl.ds(base, tt), :]
                acc = acc + rows * cw_ref[j:j + 1, :]
    y = acc + cb_ref[...]
    mu = jnp.mean(y, axis=-1, keepdims=True)
    yc = y - mu
    var = jnp.mean(yc * yc, axis=-1, keepdims=True)
    yn = yc * lax.rsqrt(var + EPS) * lng_ref[...] + lnb_ref[...]
    act = (yn * jax.nn.sigmoid(yn)).astype(_BF16)
    oc = (_dot(act, pw_ref[...]) * gconv).astype(_BF16)

    ops = []
    for gi, w in enumerate(POOL_WINDOWS):
        cols = slice(gi * POOL_GROUP, (gi + 1) * POOL_GROUP)
        win = pp_s[pl.ds(POOL_HALO, tt), cols]
        for i in range(1, w):
            win = win + pp_s[pl.ds(POOL_HALO - i, tt), cols]
        cnt = jnp.minimum(pos + 1, w).astype(_F32)
        r = (win / cnt - pin[:, cols]).astype(_BF16)
        yp = _dot(r, poolw_ref[gi]) * pscale_ref[:, cols]
        ops.append((yp * gpool[:, cols]).astype(_BF16))
    return oc, jnp.concatenate(ops, axis=1)


def _mixers(u, ust, pin, pst, gates, cw, cb, lng, lnb, pw, poolw, pscale, *, tt, q_pos0):
    b, t, _ = u.shape
    grid = (b, t // tt)
    cur = lambda w: pl.BlockSpec((None, tt, w), lambda i, j: (i, j, 0))

    def prev(rows, src_rows):
        if src_rows < rows:
            return None
        per = tt // rows
        return pl.BlockSpec((None, rows, CONV_WIDTH), lambda i, j: (i, jnp.maximum(j * per - 1, 0), 0))

    uprev_spec = prev(CONV_HALO, t)
    pprev_spec = prev(POOL_HALO, t)
    state = lambda rows: pl.BlockSpec((None, rows, CONV_WIDTH), lambda i, j: (i, 0, 0))
    uprev_src, pprev_src = u, pin
    if uprev_spec is None:
        assert t == tt
        uprev_src, uprev_spec = ust, state(CONV_HALO)
    if pprev_spec is None:
        assert t == tt
        pprev_src, pprev_spec = pst, state(POOL_HALO)
    gate_col = lambda blk: pl.BlockSpec((None, tt, CONV_WIDTH), lambda i, j: (i, j, blk))
    kern = functools.partial(_mix_kernel, tt=tt, q_pos0=q_pos0)
    return pl.pallas_call(
        kern, grid=grid,
        in_specs=[cur(CONV_WIDTH), uprev_spec, state(CONV_HALO), cur(POOL_WIDTH), pprev_spec, state(POOL_HALO),
                  gate_col(ATT_WIDTH // CONV_WIDTH), gate_col(ATT_WIDTH // CONV_WIDTH + 1),
                  _const_spec(cw.shape), _const_spec(cb.shape), _const_spec(lng.shape), _const_spec(lnb.shape),
                  _const_spec(pw.shape), _const_spec(poolw.shape), _const_spec(pscale.shape)],
        out_specs=[cur(CONV_WIDTH), cur(POOL_WIDTH)],
        out_shape=[jax.ShapeDtypeStruct((b, t, CONV_WIDTH), _BF16), jax.ShapeDtypeStruct((b, t, POOL_WIDTH), _BF16)],
        scratch_shapes=[pltpu.VMEM((CONV_HALO + tt, CONV_WIDTH), _F32), pltpu.VMEM((POOL_HALO + tt, POOL_WIDTH), _F32),
                        pltpu.VMEM((SUBLANES - 1, tt + CONV_HALO - SUBLANES, CONV_WIDTH), _F32)],
        compiler_params=pltpu.CompilerParams(dimension_semantics=("arbitrary", "arbitrary"),
                                             vmem_limit_bytes=VMEM_LIMIT),
        name="mixers",
    )(u, uprev_src, ust, pin, pprev_src, pst, gates, gates, cw, cb, lng, lnb, pw, poolw, pscale)


def _outproj_kernel(x_ref, a_ref, c_ref, pl_ref, p_ref, woa_ref, woc_ref, wop_ref, nple_ref,
                    wgate_ref, wproj_ref, nfin_ref, o_ref, *, final):
    x1 = (x_ref[...] + _dot(a_ref[...], woa_ref[...]) + _dot(c_ref[...], woc_ref[...])
          + _dot(pl_ref[...], wop_ref[...]))
    xn = _rms(x1, nple_ref[...]).astype(_BF16)
    gate = jax.nn.sigmoid(_dot(xn, wgate_ref[...]))
    x2 = x1 + gate * _dot(p_ref[...].astype(_BF16), wproj_ref[...])
    o_ref[...] = _rms(x2, nfin_ref[...]) if final else x2


def _outproj(x2, a, c, po, p_all, layer, w_out, nple, wgate, wproj, nfin, *, tm, final):
    n = x2.shape[0]
    row = lambda w: pl.BlockSpec((tm, w), lambda i: (i, 0))
    wblk = lambda rows, blk: pl.BlockSpec((rows, D_MODEL), lambda i: (blk, 0), pipeline_mode=pl.Buffered(1))
    p_spec = pl.BlockSpec((None, tm, PLE_DIM), lambda i: (layer, i, 0))
    return pl.pallas_call(
        functools.partial(_outproj_kernel, final=final), grid=(n // tm,),
        in_specs=[row(D_MODEL), row(ATT_WIDTH), row(CONV_WIDTH), row(POOL_WIDTH), p_spec,
                  wblk(ATT_WIDTH, 0), wblk(CONV_WIDTH, ATT_WIDTH // CONV_WIDTH),
                  wblk(POOL_WIDTH, ATT_WIDTH // CONV_WIDTH + 1),
                  _const_spec(nple.shape), _const_spec(wgate.shape), _const_spec(wproj.shape),
                  _const_spec(nfin.shape)],
        out_specs=row(D_MODEL),
        out_shape=jax.ShapeDtypeStruct((n, D_MODEL), _F32),
        compiler_params=pltpu.CompilerParams(dimension_semantics=("arbitrary",),
                                             vmem_limit_bytes=VMEM_LIMIT),
        name="outproj",
    )(x2, a, c, po, p_all, w_out, w_out, w_out, nple, wgate, wproj, nfin)


def _rope_tables(pos):
    pos = pos.astype(_F32)[:, None]

    def cs(half):
        freq = ROPE_THETA ** (-jnp.arange(half, dtype=_F32) / half)
        ang = pos * freq[None, :]
        return jnp.cos(ang), jnp.sin(ang)

    c, s = cs(HEAD_DIM // 2)
    c128 = jnp.concatenate([c, c], axis=-1)
    s128 = jnp.concatenate([-s, s], axis=-1)
    c, s = cs(IDX_DIM // 2)
    z = jnp.zeros_like(s)
    c64 = jnp.concatenate([c, c, c, c], axis=-1)
    s64a = jnp.concatenate([-s, z, -s, z], axis=-1)
    s64b = jnp.concatenate([z, s, z, s], axis=-1)
    return c128, s128, c64, s64a, s64b


def _split_w_in(w):
    cols = lambda a, n: w[:, a:a + n].astype(_BF16)
    pad = jnp.zeros((w.shape[0], LANES - IDX_DIM - N_IDX_HEADS), _BF16)
    wqk = jnp.concatenate([cols(_OFF_Q, ATT_WIDTH), cols(_OFF_K, KV_WIDTH)], axis=1)
    wv = cols(_OFF_V, KV_WIDTH)
    widx = jnp.concatenate([cols(_OFF_QI, IDX_WIDTH), cols(_OFF_KI, IDX_DIM), cols(_OFF_WI, N_IDX_HEADS), pad], axis=1)
    wgate = jnp.concatenate([cols(_OFF_GATT, ATT_WIDTH), cols(_OFF_GCONV, CONV_WIDTH), cols(_OFF_GPOOL, POOL_WIDTH)], axis=1)
    wglu = cols(_OFF_GLU, 2 * CONV_WIDTH)
    wpool = cols(_OFF_POOL, POOL_WIDTH)
    return wqk, wv, widx, wgate, wglu, wpool


def _row_tile(n, pref):
    t = min(pref, n)
    while n % t:
        t //= 2
    return t


def _layer(x, p_all, kv_prev, layer, pos0, k_past, v_past, ki_past, conv_prev, pool_prev, topk, lw, nfin, final):
    b, t, _ = x.shape
    n = b * t
    (norm_mix, ws, cw, cb, lng, lnb, pw, poolw, pscale, w_out, nple, wgate, wproj) = lw

    tm = _row_tile(t, 256) if b > 1 and t >= 256 else n
    if tm <= t:
        tabs = _rope_tables(pos0 + jnp.arange(t))
        tiles_per_seq = t // tm
    else:
        tabs = tuple(jnp.tile(a, (b, 1)) for a in _rope_tables(pos0 + jnp.arange(t)))
        tiles_per_seq = 1
    ust = jnp.pad(conv_prev, ((0, 0), (CONV_HALO - (CONV_K - 1), 0), (0, 0)))
    pst = jnp.pad(pool_prev, ((0, 0), (POOL_HALO - POOL_HIST, 0), (0, 0)))
    fused = CONV_HALO <= tm <= t
    mix_ops = (ust, pst, cw, cb, lng, lnb, pw, poolw, pscale) if fused else None
    (q, k_all, v_all, kb, vb, qi, ki, kib, wi, gates, u, pin, *mixed) = _inproj(
        x.reshape(n, D_MODEL), norm_mix, tabs, ws, kv_prev, mix_ops, layer=layer, depth=p_all.shape[0], tm=tm,
        tiles_per_seq=tiles_per_seq, q_pos0=pos0)
    r3 = lambda a: a.reshape(b, t, a.shape[-1])

    kb3, vb3, kib3 = r3(kb), r3(vb), r3(kib)
    if k_past is not None:
        past = k_past.shape[1]
        kb3 = jnp.concatenate([k_past.reshape(b, past, KV_WIDTH).astype(_BF16), kb3], axis=1)
        vb3 = jnp.concatenate([v_past.reshape(b, past, KV_WIDTH).astype(_BF16), vb3], axis=1)
        kib3 = jnp.concatenate([ki_past.astype(_BF16), kib3], axis=1)
    n_valid = kb3.shape[1]
    s_pad = -(-n_valid // (2 * LANES)) * (2 * LANES)
    if s_pad != n_valid:
        padk = lambda a: jnp.pad(a, ((0, 0), (0, s_pad - n_valid), (0, 0)))
        kb3, vb3, kib3 = padk(kb3), padk(vb3), padk(kib3)
    o_att = _attention(r3(q), r3(qi), r3(wi), r3(gates), kb3, vb3, kib3, q_pos0=pos0, n_valid=n_valid,
                       topk=topk, causal_tiles=k_past is None and t % (2 * LANES) == 0)

    if fused:
        oc, op = mixed
        u3, pin3 = u, pin
    else:
        u3, pin3 = r3(u), r3(pin)
        oc, op = _mixers(u3, ust, pin3, pst, r3(gates), cw, cb, lng, lnb, pw, poolw, pscale,
                         tt=_row_tile(t, 256), q_pos0=pos0)

    tm2 = _row_tile(n, 512)
    x_new = _outproj(x.reshape(n, D_MODEL), o_att.reshape(n, ATT_WIDTH), oc.reshape(n, CONV_WIDTH),
                     op.reshape(n, POOL_WIDTH), p_all.reshape(-1, n, PLE_DIM), layer, w_out, nple, wgate, wproj,
                     nfin, tm=tm2, final=final).reshape(b, t, D_MODEL)

    conv_state = jnp.concatenate([conv_prev, u3], axis=1)[:, -(CONV_K - 1):]
    pool_state = jnp.concatenate([pool_prev, pin3], axis=1)[:, -POOL_HIST:]
    return x_new, (k_all, v_all), r3(ki), conv_state, pool_state


def kernel(x_prompt, x_sample, p_prompt, p_sample, cache_k, cache_v, cache_kidx, state_conv, state_pool, norm_mix, w_in, conv_w, conv_b, conv_ln_g, conv_ln_b, conv_pw, pool_w, pool_scale, w_out, norm_ple, w_ple_gate, w_ple_proj, norm_final):
    bp, tp, _ = x_prompt.shape
    bs, ts, _ = x_sample.shape
    depth = w_in.shape[0]
    past = cache_k.shape[2]
    topk_p = min(TOPK_MAX, tp // 4)
    topk_s = min(TOPK_MAX, (past + ts) // 4)
    conv0 = jnp.zeros((bp, CONV_K - 1, CONV_WIDTH), x_prompt.dtype)
    pool0 = jnp.zeros((bp, POOL_HIST, POOL_WIDTH), x_prompt.dtype)
    nfin = norm_final.reshape(1, D_MODEL)
    hp, hs = x_prompt, x_sample
    kv_p = kv_s = None
    outs_p, outs_s = [], []
    for i in range(depth):
        vec = lambda a: a[i].reshape(1, -1)
        lw = (vec(norm_mix), _split_w_in(w_in[i]), conv_w[i], vec(conv_b), vec(conv_ln_g), vec(conv_ln_b),
              conv_pw[i].astype(_BF16), pool_w[i].astype(_BF16), vec(pool_scale), w_out[i].astype(_BF16),
              vec(norm_ple), w_ple_gate[i].astype(_BF16), w_ple_proj[i].astype(_BF16))
        final = i == depth - 1
        hp, kv_p, *op_ = _layer(hp, p_prompt, kv_p, i, 0, None, None, None, conv0, pool0, topk_p, lw, nfin, final)
        hs, kv_s, *os_ = _layer(hs, p_sample, kv_s, i, past, cache_k[i], cache_v[i], cache_kidx[i],
                                state_conv[i], state_pool[i], topk_s, lw, nfin, final)
        outs_p.append(op_)
        outs_s.append(os_)
    stack = lambda outs, j: jnp.stack([o[j] for o in outs])
    heads = lambda a, b, t: a.reshape(depth, b, t, N_KV_HEADS, HEAD_DIM)
    return ((hp, hs, heads(kv_p[0], bp, tp), heads(kv_p[1], bp, tp)) + tuple(stack(outs_p, j) for j in range(3))
            + (heads(kv_s[0], bs, ts), heads(kv_s[1], bs, ts)) + tuple(stack(outs_s, j) for j in range(3)))
```

```python
import functools

import numpy as np
import jax
import jax.numpy as jnp
from jax import lax
from jax.experimental import pallas as pl
from jax.experimental.pallas import tpu as pltpu

D_MODEL = 2048
CHUNK = 64
HEAD_DIM = 128
ATT_WIDTH = D_MODEL // 2
CONV_WIDTH = D_MODEL // 4
POOL_WIDTH = D_MODEL // 4
N_HEADS = ATT_WIDTH // HEAD_DIM
N_KV_HEADS = 2
HEADS_PER_KV = N_HEADS // N_KV_HEADS
KV_WIDTH = N_KV_HEADS * HEAD_DIM
N_IDX_HEADS = 16
IDX_DIM = 64
IDX_WIDTH = N_IDX_HEADS * IDX_DIM
TOPK_MAX = 256
ROPE_THETA = 10000.0
CONV_K = 31
CONV_HALO = 32
POOL_WINDOWS = (2, 4, 8, 16)
POOL_GROUP = POOL_WIDTH // len(POOL_WINDOWS)
POOL_HIST = 15
POOL_HALO = 16
PLE_DIM = 256
EPS = 1e-6
LANES = 128
SUBLANES = 8
NEG_BIG = -1e30
VMEM_LIMIT = 60 * 1024 * 1024
N_INPROJ_OUTS = 12
N_MIX_IN = 9
N_MIX_TAIL = 5
ATTN_TQ = 256
ATTN_SUB = 128
ATTN_STEP_ELEMS = 2 * 256 * 1024
ATTN_STEP_ROWS = 512
ATTN_MAX_CHAINS = 8

_OFF_Q = 0
_OFF_K = _OFF_Q + ATT_WIDTH
_OFF_V = _OFF_K + KV_WIDTH
_OFF_QI = _OFF_V + KV_WIDTH
_OFF_KI = _OFF_QI + IDX_WIDTH
_OFF_WI = _OFF_KI + IDX_DIM
_OFF_GATT = _OFF_WI + N_IDX_HEADS
_OFF_GLU = _OFF_GATT + ATT_WIDTH
_OFF_GCONV = _OFF_GLU + 2 * CONV_WIDTH
_OFF_POOL = _OFF_GCONV + CONV_WIDTH
_OFF_GPOOL = _OFF_POOL + POOL_WIDTH
N_IN = _OFF_GPOOL + POOL_WIDTH

_F32 = jnp.float32
_BF16 = jnp.bfloat16


def _const_spec(shape):
    nd = len(shape)
    return pl.BlockSpec(shape, lambda *_: (0,) * nd, pipeline_mode=pl.Buffered(1))


def _dot(a, b):
    return jnp.dot(a, b, preferred_element_type=_F32)


def _dot_nt(a, b):
    return lax.dot_general(a, b, (((1,), (1,)), ((), ())), preferred_element_type=_F32)


def _rms(x, g):
    return x * lax.rsqrt(jnp.mean(x * x, axis=-1, keepdims=True) + EPS) * g


def _inproj_kernel(x_ref, g_ref, c128_ref, s128_ref, c64_ref, s64a_ref, s64b_ref,
                   wqk_ref, wv_ref, widx_ref, wgate_ref, wglu_ref, wpool_ref, *rest, first_plane, mix):
    tm = x_ref.shape[0]
    if mix is not None:
        (ust_ref, pst_ref, cw_ref, cb_ref, lng_ref, lnb_ref, pw_ref, poolw_ref, pscale_ref) = rest[:N_MIX_IN]
        oc_ref, op_ref, xp_s, pp_s, sh_s = rest[-N_MIX_TAIL:]
        rest = rest[N_MIX_IN:-N_MIX_TAIL]
        t = lax.rem(pl.program_id(0), mix[1])

        @pl.when(t == 0)
        def _():
            xp_s[0:CONV_HALO, :] = ust_ref[...]
            pp_s[0:POOL_HALO, :] = pst_ref[...]

        @pl.when(t > 0)
        def _():
            xp_s[0:CONV_HALO, :] = xp_s[tm:tm + CONV_HALO, :]
            pp_s[0:POOL_HALO, :] = pp_s[tm:tm + POOL_HALO, :]

    (q_ref, k_ref, v_ref, kb_ref, vb_ref, qi_ref, ki_ref, kib_ref, wi_ref,
     gates_ref, u_ref, pin_ref) = rest[-N_INPROJ_OUTS:]
    if first_plane is not None:
        for ref in (k_ref, v_ref):
            for d in range(ref.shape[0]):
                if d != first_plane:
                    ref[d] = jnp.zeros(ref.shape[1:], _F32)
        k_ref, v_ref = k_ref.at[first_plane], v_ref.at[first_plane]
    xn = _rms(x_ref[...], g_ref[...]).astype(_BF16)

    zg = _dot(xn, wgate_ref[...])
    gates = zg * jax.nn.sigmoid(zg)
    gates_ref[...] = gates.astype(_BF16)

    zu = _dot(xn, wglu_ref[...])
    u = zu[:, :CONV_WIDTH] * jax.nn.sigmoid(zu[:, CONV_WIDTH:])
    pin = _dot(xn, wpool_ref[...])
    if mix is None:
        u_ref[...] = u
        pin_ref[...] = pin
    else:
        pos = mix[0] + t * tm + lax.broadcasted_iota(jnp.int32, (tm, 1), 0)
        oc, op = _mix_compute(u, pin, gates[:, ATT_WIDTH:ATT_WIDTH + CONV_WIDTH], gates[:, ATT_WIDTH + CONV_WIDTH:],
                              pos, cw_ref, cb_ref, lng_ref, lnb_ref, pw_ref, poolw_ref, pscale_ref, xp_s, pp_s, sh_s)
        oc_ref[...] = oc
        op_ref[...] = op
        u_ref[...] = u[tm - CONV_HALO:, :]
        pin_ref[...] = pin[tm - POOL_HALO:, :]

    zqk = _dot(xn, wqk_ref[...])
    c128 = c128_ref[...]
    s128 = s128_ref[...]
    for h in range(N_HEADS + N_KV_HEADS):
        xh = zqk[:, h * HEAD_DIM:(h + 1) * HEAD_DIM]
        r = xh * c128 + pltpu.roll(xh, HEAD_DIM // 2, 1) * s128
        if h < N_HEADS:
            q_ref[:, h * HEAD_DIM:(h + 1) * HEAD_DIM] = (r * (HEAD_DIM ** -0.5)).astype(_BF16)
        else:
            j = h - N_HEADS
            k_ref[pl.ds(j, tm, stride=N_KV_HEADS), :] = r
            kb_ref[:, j * HEAD_DIM:(j + 1) * HEAD_DIM] = r.astype(_BF16)

    zv = _dot(xn, wv_ref[...])
    for j in range(N_KV_HEADS):
        v_ref[pl.ds(j, tm, stride=N_KV_HEADS), :] = zv[:, j * HEAD_DIM:(j + 1) * HEAD_DIM]
    vb_ref[...] = zv.astype(_BF16)

    zi = _dot(xn, widx_ref[...])
    c64 = c64_ref[...]
    s64a = s64a_ref[...]
    s64b = s64b_ref[...]
    n_groups = IDX_WIDTH // LANES
    for j in range(n_groups + 1):
        xj = zi[:, j * LANES:(j + 1) * LANES]
        r = (xj * c64 + pltpu.roll(xj, LANES - IDX_DIM // 2, 1) * s64a
             + pltpu.roll(xj, IDX_DIM // 2, 1) * s64b)
        if j < n_groups:
            qi_ref[:, j * LANES:(j + 1) * LANES] = r.astype(_BF16)
        else:
            ki_ref[...] = r[:, :IDX_DIM]
            kib_ref[...] = r[:, :IDX_DIM].astype(_BF16)
            wi_ref[...] = xj[:, IDX_DIM:IDX_DIM + N_IDX_HEADS] * (IDX_DIM ** -0.5 * N_IDX_HEADS ** -0.5)


def _inproj(x2, norm, tabs, ws, kv_prev, mix_ops, *, layer, depth, tm, tiles_per_seq, q_pos0):
    n = x2.shape[0]
    grid = (n // tm,)
    row = lambda w: pl.BlockSpec((tm, w), lambda i: (i, 0))
    tab = pl.BlockSpec((tm, LANES), lambda i: (i % tiles_per_seq, 0))
    kv_shape = jax.ShapeDtypeStruct((depth, n * N_KV_HEADS, HEAD_DIM), _F32)
    if kv_prev is None:
        kv_spec = pl.BlockSpec((depth, tm * N_KV_HEADS, HEAD_DIM), lambda i: (0, i, 0))
    else:
        kv_spec = pl.BlockSpec((None, tm * N_KV_HEADS, HEAD_DIM), lambda i: (layer, i, 0))
    out_shape = [
        jax.ShapeDtypeStruct((n, ATT_WIDTH), _BF16),
        kv_shape,
        kv_shape,
        jax.ShapeDtypeStruct((n, KV_WIDTH), _BF16),
        jax.ShapeDtypeStruct((n, KV_WIDTH), _BF16),
        jax.ShapeDtypeStruct((n, IDX_WIDTH), _BF16),
        jax.ShapeDtypeStruct((n, IDX_DIM), _F32),
        jax.ShapeDtypeStruct((n, IDX_DIM), _BF16),
        jax.ShapeDtypeStruct((n, N_IDX_HEADS), _F32),
        jax.ShapeDtypeStruct((n, D_MODEL), _BF16),
        jax.ShapeDtypeStruct((n, CONV_WIDTH), _F32),
        jax.ShapeDtypeStruct((n, POOL_WIDTH), _F32),
    ]
    assert len(out_shape) == N_INPROJ_OUTS
    out_specs = [kv_spec if s is kv_shape else row(s.shape[1]) for s in out_shape]
    in_specs = ([row(D_MODEL), _const_spec((1, D_MODEL))] + [tab] * 5
                + [_const_spec(w.shape) for w in ws])
    args = (x2, norm, *tabs, *ws)
    scratch = []
    if mix_ops is not None:
        assert len(mix_ops) == N_MIX_IN and tm >= CONV_HALO
        n_seq = n // (tm * tiles_per_seq)
        seq = lambda rows: pl.BlockSpec((None, rows, CONV_WIDTH), lambda i: (i // tiles_per_seq, 0, 0))
        in_specs += [seq(CONV_HALO), seq(POOL_HALO)] + [_const_spec(w.shape) for w in mix_ops[2:]]
        args += tuple(mix_ops)
        out_shape[-2:] = [jax.ShapeDtypeStruct((n_seq, CONV_HALO, CONV_WIDTH), _F32),
                          jax.ShapeDtypeStruct((n_seq, POOL_HALO, POOL_WIDTH), _F32)]
        out_specs[-2:] = [seq(CONV_HALO), seq(POOL_HALO)]
        out_shape += [jax.ShapeDtypeStruct((n, CONV_WIDTH), _BF16), jax.ShapeDtypeStruct((n, POOL_WIDTH), _BF16)]
        out_specs += [row(CONV_WIDTH), row(POOL_WIDTH)]
        scratch = [pltpu.VMEM((CONV_HALO + tm, CONV_WIDTH), _F32), pltpu.VMEM((POOL_HALO + tm, POOL_WIDTH), _F32),
                   pltpu.VMEM((SUBLANES - 1, tm + CONV_HALO - SUBLANES, CONV_WIDTH), _F32)]
        assert len(scratch) + 2 == N_MIX_TAIL
    aliases = {}
    if kv_prev is not None:
        aliases = {len(args): 1, len(args) + 1: 2}
        in_specs = in_specs + [pl.BlockSpec(memory_space=pl.ANY)] * 2
        args = args + tuple(kv_prev)
    return pl.pallas_call(
        functools.partial(_inproj_kernel, first_plane=layer if kv_prev is None else None,
                          mix=None if mix_ops is None else (q_pos0, tiles_per_seq)),
        grid=grid, in_specs=in_specs, out_specs=out_specs, out_shape=out_shape, scratch_shapes=scratch,
        input_output_aliases=aliases,
        compiler_params=pltpu.CompilerParams(dimension_semantics=("arbitrary",),
                                             vmem_limit_bytes=VMEM_LIMIT),
        name="inproj",
    )(*args)


def _softmax_attention(q_ref, gate_ref, kb_ref, vb_ref, bias_s, o_ref, *, tq, sub, s_len):
    for r0 in range(0, tq, sub):
        rows = slice(r0, r0 + sub)
        q = q_ref[rows, :]
        gate = gate_ref[rows, :].astype(_F32)
        bias = bias_s[rows, :]
        for g in range(N_KV_HEADS):
            heads = [g * HEADS_PER_KV + r for r in range(HEADS_PER_KV)]
            qg = jnp.concatenate([q[:, h * HEAD_DIM:(h + 1) * HEAD_DIM] for h in heads], axis=0)
            s = _dot_nt(qg, kb_ref[:, g * HEAD_DIM:(g + 1) * HEAD_DIM])
            s = (s.reshape(HEADS_PER_KV, sub, s_len) + bias[None]).reshape(HEADS_PER_KV * sub, s_len)
            m = jnp.max(s, axis=-1, keepdims=True)
            p = jnp.exp(s - m)
            den = jnp.sum(p, axis=-1, keepdims=True)
            o = _dot(p.astype(_BF16), vb_ref[:, g * HEAD_DIM:(g + 1) * HEAD_DIM]) / den
            for r, h in enumerate(heads):
                cols = slice(h * HEAD_DIM, (h + 1) * HEAD_DIM)
                o_ref[rows, cols] = (o[r * sub:(r + 1) * sub] * gate[:, cols]).astype(_BF16)


def _visible(tq, cols, col0, q_pos0, n_valid):
    qpos = q_pos0 + lax.broadcasted_iota(jnp.int32, (tq, 1), 0)
    kpos = col0 + lax.broadcasted_iota(jnp.int32, (1, cols), 1)
    return (lax.div(kpos, jnp.int32(CHUNK)) <= lax.div(qpos, jnp.int32(CHUNK))) & (kpos < n_valid)


def _attn_dense_kernel(q_ref, gate_ref, kb_ref, vb_ref, o_ref, bias_s, *, tq, sub, s_len, q_pos0, n_valid):
    bias_s[...] = jnp.where(_visible(tq, s_len, 0, q_pos0, n_valid), 0.0, NEG_BIG)
    if o_ref.shape[0] > tq:
        o_ref[tq:, :] = jnp.zeros((o_ref.shape[0] - tq, o_ref.shape[1]), o_ref.dtype)
    _softmax_attention(q_ref, gate_ref, kb_ref, vb_ref, bias_s, o_ref, tq=tq, sub=sub, s_len=s_len)


def _attn_kernel(q_ref, gate_ref, kb_ref, vb_ref, qi0_ref, wi0_ref, kib0_ref, qi1_ref, wi1_ref, kib1_ref,
                 obuf_ref, o_ref, score_s, acc_s, bias_s, qi_s, w_s, *, nb, tq, sub, s_len, q_pos0, n_valid, topk):
    del obuf_ref
    b = pl.program_id(0)
    slot = lax.rem(b, 2)
    nslot = 1 - slot
    half = s_len // 2
    nblk = half // LANES
    n_slices = 2 * N_IDX_HEADS
    assert n_slices == 32
    tiles = range(nb)

    qpos = q_pos0 + lax.broadcasted_iota(jnp.int32, (tq, 1), 0)
    n_vis = jnp.minimum((lax.div(qpos, jnp.int32(CHUNK)) + 1) * CHUNK, n_valid)
    vis_half = lambda hf: _visible(tq, half, hf * half, q_pos0, n_valid)

    @pl.when(b == 0)
    def _():
        for bb in tiles:
            qi = qi0_ref[bb]
            wi = wi0_ref[bb]
            for hf in range(2):
                kib = kib0_ref[bb, hf * half:(hf + 1) * half, :]
                acc = jnp.zeros((tq, half), _F32)
                for h in range(N_IDX_HEADS):
                    l = _dot_nt(qi[:, h * IDX_DIM:(h + 1) * IDX_DIM], kib)
                    acc = acc + jnp.maximum(l, 0.0) * wi[:, h:h + 1]
                score_s[bb, 0, hf] = jnp.where(vis_half(hf), acc, -jnp.inf)

    for bb in tiles:
        qi1 = qi1_ref[bb]
        wi1 = wi1_ref[bb]
        for h in range(N_IDX_HEADS):
            qi_s[bb, h] = qi1[:, h * IDX_DIM:(h + 1) * IDX_DIM]
            w_s[bb, h] = jnp.broadcast_to(wi1[:, h:h + 1], (tq, LANES))
    acc_s[...] = jnp.zeros((nb, 2, tq, half), _F32)

    int_min = jnp.int32(-2 ** 31)

    def decode(u):
        sk = u ^ int_min
        bits = jnp.where(sk < 0, sk ^ jnp.int32(0x7FFFFFFF), sk)
        return lax.bitcast_convert_type(bits, _F32)

    def count(bb, pred):
        return (jnp.sum(pred(score_s[bb, slot, 0]).astype(_F32), axis=-1, keepdims=True)
                + jnp.sum(pred(score_s[bb, slot, 1]).astype(_F32), axis=-1, keepdims=True))

    def step(i, us):
        h = lax.div(i, jnp.int32(2))
        hf = lax.rem(i, jnp.int32(2))
        out = []
        for bb, u in zip(tiles, us):
            cand = u | lax.shift_left(jnp.int32(1), 31 - i)
            t = decode(cand)
            out.append(jnp.where(count(bb, lambda s: s >= t) >= topk, cand, u))
            l = _dot_nt(qi_s[bb, h], kib1_ref[bb, pl.ds(pl.multiple_of(hf * half, LANES), half), :])
            wt = w_s[bb, h]
            term = jnp.concatenate([jnp.maximum(l[:, j * LANES:(j + 1) * LANES], 0.0) * wt for j in range(nblk)],
                                   axis=1)
            acc_s[bb, hf] = acc_s[bb, hf] + term
        return tuple(out)

    us = lax.fori_loop(0, n_slices, step, tuple(jnp.zeros((tq, 1), jnp.int32) for _ in tiles))
    thrs, flags = [], []
    for bb, u in zip(tiles, us):
        for hf in range(2):
            score_s[bb, nslot, hf] = jnp.where(vis_half(hf), acc_s[bb, hf], -jnp.inf)
        thr = jnp.where(n_vis <= topk, -jnp.inf, decode(u))
        n_ge = count(bb, lambda s: s >= thr)
        for hf in range(2):
            keep = (score_s[bb, slot, hf] >= thr) & vis_half(hf)
            bias_s[bb, :, hf * half:(hf + 1) * half] = jnp.where(keep, 0.0, NEG_BIG)
        thrs.append(thr)
        flags.append(jnp.max(jnp.where((n_ge > topk) & (n_vis > topk), 1.0, 0.0)))

    @pl.when(functools.reduce(jnp.maximum, flags) > 0.0)
    def _():
        earlier = (lax.broadcasted_iota(jnp.int32, (LANES, LANES), 0)
                   < lax.broadcasted_iota(jnp.int32, (LANES, LANES), 1)).astype(_BF16)
        for bb, thr in zip(tiles, thrs):
            need = topk - count(bb, lambda s: s > thr)
            run = jnp.zeros((tq, 1), _F32)
            for hf in range(2):
                for j in range(nblk):
                    c0 = hf * half + j * LANES
                    sc = score_s[bb, slot, hf, :, j * LANES:(j + 1) * LANES]
                    vis = _visible(tq, LANES, c0, q_pos0, n_valid)
                    eq = (sc == thr) & vis
                    e = eq.astype(_F32)
                    rank = _dot(e.astype(_BF16), earlier) + run
                    pick = ((sc > thr) & vis) | (eq & (rank < need))
                    bias_s[bb, :, c0:c0 + LANES] = jnp.where(pick, 0.0, NEG_BIG)
                    run = run + jnp.sum(e, axis=-1, keepdims=True)

    for bb in tiles:
        _softmax_attention(q_ref.at[bb], gate_ref.at[bb], kb_ref.at[bb], vb_ref.at[bb], bias_s.at[bb], o_ref.at[bb],
                           tq=tq, sub=sub, s_len=s_len)


def _attention_tile(obuf, q, qi, wi, gates, kb, vb, kib, *, tq, qblk, s_len, q_pos0, n_valid, topk):
    b, t, _ = q.shape
    sub = min(tq, ATTN_SUB)
    params = pltpu.CompilerParams(dimension_semantics=("arbitrary",), vmem_limit_bytes=VMEM_LIMIT)
    cur = lambda w: pl.BlockSpec((None, tq, w), lambda i: (i, qblk, 0))
    kcur = lambda w: pl.BlockSpec((None, s_len, w), lambda i: (i, 0, 0))
    out_shape = jax.ShapeDtypeStruct((b, t, ATT_WIDTH), _BF16)
    if s_len <= topk:
        assert obuf is None
        kern = functools.partial(_attn_dense_kernel, tq=tq, sub=sub, s_len=s_len, q_pos0=q_pos0, n_valid=n_valid)
        return pl.pallas_call(
            kern, grid=(b,), in_specs=[cur(ATT_WIDTH), cur(ATT_WIDTH), kcur(KV_WIDTH), kcur(KV_WIDTH)],
            out_specs=pl.BlockSpec((None, t, ATT_WIDTH), lambda i: (i, 0, 0)), out_shape=out_shape,
            scratch_shapes=[pltpu.VMEM((tq, s_len), _F32)], compiler_params=params, name="attention_dense",
        )(q, gates, kb, vb)
    if obuf is None:
        obuf = jnp.zeros(out_shape.shape, _BF16)
    nb = 1
    while (nb < ATTN_MAX_CHAINS and b % (2 * nb) == 0 and 2 * nb * tq <= ATTN_STEP_ROWS
           and 2 * nb * tq * s_len <= ATTN_STEP_ELEMS):
        nb *= 2
    steps = b // nb
    ahead = lambda i: jnp.minimum(i + 1, steps - 1)
    cur = lambda w: pl.BlockSpec((nb, tq, w), lambda i: (i, qblk, 0))
    kcur = lambda w: pl.BlockSpec((nb, s_len, w), lambda i: (i, 0, 0))
    nxt = lambda w: pl.BlockSpec((nb, tq, w), lambda i: (ahead(i), qblk, 0))
    knxt = lambda w: pl.BlockSpec((nb, s_len, w), lambda i: (ahead(i), 0, 0))
    kern = functools.partial(_attn_kernel, nb=nb, tq=tq, sub=sub, s_len=s_len, q_pos0=q_pos0, n_valid=n_valid,
                             topk=topk)
    half = s_len // 2
    return pl.pallas_call(
        kern, grid=(steps,),
        in_specs=[cur(ATT_WIDTH), cur(ATT_WIDTH), kcur(KV_WIDTH), kcur(KV_WIDTH),
                  cur(IDX_WIDTH), cur(N_IDX_HEADS), kcur(IDX_DIM),
                  nxt(IDX_WIDTH), nxt(N_IDX_HEADS), knxt(IDX_DIM),
                  pl.BlockSpec(memory_space=pl.ANY)],
        out_specs=cur(ATT_WIDTH), out_shape=out_shape, input_output_aliases={10: 0},
        scratch_shapes=[pltpu.VMEM((nb, 2, 2, tq, half), _F32), pltpu.VMEM((nb, 2, tq, half), _F32),
                        pltpu.VMEM((nb, tq, s_len), _F32),
                        pltpu.VMEM((nb, N_IDX_HEADS, tq, IDX_DIM), _BF16),
                        pltpu.VMEM((nb, N_IDX_HEADS, tq, LANES), _F32)],
        compiler_params=params, name="attention",
    )(q, gates, kb, vb, qi, wi, kib, qi, wi, kib, obuf)


def _attention(q, qi, wi, gates, kb, vb, kib, *, q_pos0, n_valid, topk, causal_tiles):
    b, t, _ = q.shape
    if causal_tiles:
        tq = _row_tile(t, ATTN_TQ)
        assert q_pos0 == 0 and tq % (2 * LANES) == 0
        obuf = None
        for j in range(t // tq):
            obuf = _attention_tile(obuf, q, qi, wi, gates, kb, vb, kib, tq=tq, qblk=j, s_len=(j + 1) * tq,
                                   q_pos0=j * tq, n_valid=n_valid, topk=topk)
        return obuf
    return _attention_tile(None, q, qi, wi, gates, kb, vb, kib, tq=t, qblk=0, s_len=kb.shape[1],
                           q_pos0=q_pos0, n_valid=n_valid, topk=topk)


def _mix_kernel(u_ref, uprev_ref, ust_ref, pin_ref, pprev_ref, pst_ref, gconv_ref, gpool_ref,
                cw_ref, cb_ref, lng_ref, lnb_ref, pw_ref, poolw_ref, pscale_ref,
                oc_ref, op_ref, xp_s, pp_s, sh_s, *, tt, q_pos0):
    t = pl.program_id(1)

    @pl.when(t == 0)
    def _():
        xp_s[0:CONV_HALO, :] = ust_ref[...]
        pp_s[0:POOL_HALO, :] = pst_ref[...]

    @pl.when(t > 0)
    def _():
        xp_s[0:CONV_HALO, :] = uprev_ref[...]
        pp_s[0:POOL_HALO, :] = pprev_ref[...]

    pos = q_pos0 + t * tt + lax.broadcasted_iota(jnp.int32, (tt, 1), 0)
    oc, op = _mix_compute(u_ref[...], pin_ref[...], gconv_ref[...].astype(_F32), gpool_ref[...].astype(_F32), pos,
                          cw_ref, cb_ref, lng_ref, lnb_ref, pw_ref, poolw_ref, pscale_ref, xp_s, pp_s, sh_s)
    oc_ref[...] = oc
    op_ref[...] = op


def _mix_compute(u, pin, gconv, gpool, pos, cw_ref, cb_ref, lng_ref, lnb_ref, pw_ref, poolw_ref, pscale_ref,
                 xp_s, pp_s, sh_s):
    tt = u.shape[0]
    xp_s[CONV_HALO:CONV_HALO + tt, :] = u
    pp_s[POOL_HALO:POOL_HALO + tt, :] = pin

    first = CONV_HALO - (CONV_K - 1)
    span = tt + CONV_HALO - SUBLANES
    acc = jnp.zeros((tt, CONV_WIDTH), _F32)
    for r in range(SUBLANES):
        if r:
            sh_s[r - 1] = xp_s[pl.ds(r, span), :]
        for j in range(CONV_K):
            if (first + j) % SUBLANES == r:
                base = first + j - r
                rows = sh_s[r - 1, pl.ds(base, tt), :] if r else xp_s[pl.ds(base, tt), :]
                acc = acc + rows * cw_ref[j:j + 1, :]
    y = acc + cb_ref[...]
    mu = jnp.mean(y, axis=-1, keepdims=True)
    yc = y - mu
    var = jnp.mean(yc * yc, axis=-1, keepdims=True)
    yn = yc * lax.rsqrt(var + EPS) * lng_ref[...] + lnb_ref[...]
    act = (yn * jax.nn.sigmoid(yn)).astype(_BF16)
    oc = (_dot(act, pw_ref[...]) * gconv).astype(_BF16)

    ops = []
    for gi, w in enumerate(POOL_WINDOWS):
        cols = slice(gi * POOL_GROUP, (gi + 1) * POOL_GROUP)
        win = pp_s[pl.ds(POOL_HALO, tt), cols]
        for i in range(1, w):
            win = win + pp_s[pl.ds(POOL_HALO - i, tt), cols]
        cnt = jnp.minimum(pos + 1, w).astype(_F32)
        r = (win / cnt - pin[:, cols]).astype(_BF16)
        yp = _dot(r, poolw_ref[gi]) * pscale_ref[:, cols]
        ops.append((yp * gpool[:, cols]).astype(_BF16))
    return oc, jnp.concatenate(ops, axis=1)


def _mixers(u, ust, pin, pst, gates, cw, cb, lng, lnb, pw, poolw, pscale, *, tt, q_pos0):
    b, t, _ = u.shape
    grid = (b, t // tt)
    cur = lambda w: pl.BlockSpec((None, tt, w), lambda i, j: (i, j, 0))

    def prev(rows, src_rows):
        if src_rows < rows:
            return None
        per = tt // rows
        return pl.BlockSpec((None, rows, CONV_WIDTH), lambda i, j: (i, jnp.maximum(j * per - 1, 0), 0))

    uprev_spec = prev(CONV_HALO, t)
    pprev_spec = prev(POOL_HALO, t)
    state = lambda rows: pl.BlockSpec((None, rows, CONV_WIDTH), lambda i, j: (i, 0, 0))
    uprev_src, pprev_src = u, pin
    if uprev_spec is None:
        assert t == tt
        uprev_src, uprev_spec = ust, state(CONV_HALO)
    if pprev_spec is None:
        assert t == tt
        pprev_src, pprev_spec = pst, state(POOL_HALO)
    gate_col = lambda blk: pl.BlockSpec((None, tt, CONV_WIDTH), lambda i, j: (i, j, blk))
    kern = functools.partial(_mix_kernel, tt=tt, q_pos0=q_pos0)
    return pl.pallas_call(
        kern, grid=grid,
        in_specs=[cur(CONV_WIDTH), uprev_spec, state(CONV_HALO), cur(POOL_WIDTH), pprev_spec, state(POOL_HALO),
                  gate_col(ATT_WIDTH // CONV_WIDTH), gate_col(ATT_WIDTH // CONV_WIDTH + 1),
                  _const_spec(cw.shape), _const_spec(cb.shape), _const_spec(lng.shape), _const_spec(lnb.shape),
                  _const_spec(pw.shape), _const_spec(poolw.shape), _const_spec(pscale.shape)],
        out_specs=[cur(CONV_WIDTH), cur(POOL_WIDTH)],
        out_shape=[jax.ShapeDtypeStruct((b, t, CONV_WIDTH), _BF16), jax.ShapeDtypeStruct((b, t, POOL_WIDTH), _BF16)],
        scratch_shapes=[pltpu.VMEM((CONV_HALO + tt, CONV_WIDTH), _F32), pltpu.VMEM((POOL_HALO + tt, POOL_WIDTH), _F32),
                        pltpu.VMEM((SUBLANES - 1, tt + CONV_HALO - SUBLANES, CONV_WIDTH), _F32)],
        compiler_params=pltpu.CompilerParams(dimension_semantics=("arbitrary", "arbitrary"),
                                             vmem_limit_bytes=VMEM_LIMIT),
        name="mixers",
    )(u, uprev_src, ust, pin, pprev_src, pst, gates, gates, cw, cb, lng, lnb, pw, poolw, pscale)


def _outproj_kernel(x_ref, a_ref, c_ref, pl_ref, p_ref, woa_ref, woc_ref, wop_ref, nple_ref,
                    wgate_ref, wproj_ref, nfin_ref, o_ref, *, final):
    x1 = (x_ref[...] + _dot(a_ref[...], woa_ref[...]) + _dot(c_ref[...], woc_ref[...])
          + _dot(pl_ref[...], wop_ref[...]))
    xn = _rms(x1, nple_ref[...]).astype(_BF16)
    gate = jax.nn.sigmoid(_dot(xn, wgate_ref[...]))
    x2 = x1 + gate * _dot(p_ref[...].astype(_BF16), wproj_ref[...])
    o_ref[...] = _rms(x2, nfin_ref[...]) if final else x2


def _outproj(x2, a, c, po, p_all, layer, w_out, nple, wgate, wproj, nfin, *, tm, final):
    n = x2.shape[0]
    row = lambda w: pl.BlockSpec((tm, w), lambda i: (i, 0))
    wblk = lambda rows, blk: pl.BlockSpec((rows, D_MODEL), lambda i: (blk, 0), pipeline_mode=pl.Buffered(1))
    p_spec = pl.BlockSpec((None, tm, PLE_DIM), lambda i: (layer, i, 0))
    return pl.pallas_call(
        functools.partial(_outproj_kernel, final=final), grid=(n // tm,),
        in_specs=[row(D_MODEL), row(ATT_WIDTH), row(CONV_WIDTH), row(POOL_WIDTH), p_spec,
                  wblk(ATT_WIDTH, 0), wblk(CONV_WIDTH, ATT_WIDTH // CONV_WIDTH),
                  wblk(POOL_WIDTH, ATT_WIDTH // CONV_WIDTH + 1),
                  _const_spec(nple.shape), _const_spec(wgate.shape), _const_spec(wproj.shape),
                  _const_spec(nfin.shape)],
        out_specs=row(D_MODEL),
        out_shape=jax.ShapeDtypeStruct((n, D_MODEL), _F32),
        compiler_params=pltpu.CompilerParams(dimension_semantics=("arbitrary",),
                                             vmem_limit_bytes=VMEM_LIMIT),
        name="outproj",
    )(x2, a, c, po, p_all, w_out, w_out, w_out, nple, wgate, wproj, nfin)


def _rope_tables(pos):
    pos = pos.astype(_F32)[:, None]

    def cs(half):
        freq = ROPE_THETA ** (-jnp.arange(half, dtype=_F32) / half)
        ang = pos * freq[None, :]
        return jnp.cos(ang), jnp.sin(ang)

    c, s = cs(HEAD_DIM // 2)
    c128 = jnp.concatenate([c, c], axis=-1)
    s128 = jnp.concatenate([-s, s], axis=-1)
    c, s = cs(IDX_DIM // 2)
    z = jnp.zeros_like(s)
    c64 = jnp.concatenate([c, c, c, c], axis=-1)
    s64a = jnp.concatenate([-s, z, -s, z], axis=-1)
    s64b = jnp.concatenate([z, s, z, s], axis=-1)
    return c128, s128, c64, s64a, s64b


def _split_w_in(w):
    cols = lambda a, n: w[:, a:a + n].astype(_BF16)
    pad = jnp.zeros((w.shape[0], LANES - IDX_DIM - N_IDX_HEADS), _BF16)
    wqk = jnp.concatenate([cols(_OFF_Q, ATT_WIDTH), cols(_OFF_K, KV_WIDTH)], axis=1)
    wv = cols(_OFF_V, KV_WIDTH)
    widx = jnp.concatenate([cols(_OFF_QI, IDX_WIDTH), cols(_OFF_KI, IDX_DIM), cols(_OFF_WI, N_IDX_HEADS), pad], axis=1)
    wgate = jnp.concatenate([cols(_OFF_GATT, ATT_WIDTH), cols(_OFF_GCONV, CONV_WIDTH), cols(_OFF_GPOOL, POOL_WIDTH)], axis=1)
    wglu = cols(_OFF_GLU, 2 * CONV_WIDTH)
    wpool = cols(_OFF_POOL, POOL_WIDTH)
    return wqk, wv, widx, wgate, wglu, wpool


def _row_tile(n, pref):
    t = min(pref, n)
    while n % t:
        t //= 2
    return t


def _layer(x, p_all, kv_prev, layer, pos0, k_past, v_past, ki_past, conv_prev, pool_prev, topk, lw, nfin, final):
    b, t, _ = x.shape
    n = b * t
    (norm_mix, ws, cw, cb, lng, lnb, pw, poolw, pscale, w_out, nple, wgate, wproj) = lw

    tm = _row_tile(t, 256) if b > 1 and t >= 256 else n
    if tm <= t:
        tabs = _rope_tables(pos0 + jnp.arange(t))
        tiles_per_seq = t // tm
    else:
        tabs = tuple(jnp.tile(a, (b, 1)) for a in _rope_tables(pos0 + jnp.arange(t)))
        tiles_per_seq = 1
    ust = jnp.pad(conv_prev, ((0, 0), (CONV_HALO - (CONV_K - 1), 0), (0, 0)))
    pst = jnp.pad(pool_prev, ((0, 0), (POOL_HALO - POOL_HIST, 0), (0, 0)))
    fused = CONV_HALO <= tm <= t
    mix_ops = (ust, pst, cw, cb, lng, lnb, pw, poolw, pscale) if fused else None
    (q, k_all, v_all, kb, vb, qi, ki, kib, wi, gates, u, pin, *mixed) = _inproj(
        x.reshape(n, D_MODEL), norm_mix, tabs, ws, kv_prev, mix_ops, layer=layer, depth=p_all.shape[0], tm=tm,
        tiles_per_seq=tiles_per_seq, q_pos0=pos0)
    r3 = lambda a: a.reshape(b, t, a.shape[-1])

    kb3, vb3, kib3 = r3(kb), r3(vb), r3(kib)
    if k_past is not None:
        past = k_past.shape[1]
        kb3 = jnp.concatenate([k_past.reshape(b, past, KV_WIDTH).astype(_BF16), kb3], axis=1)
        vb3 = jnp.concatenate([v_past.reshape(b, past, KV_WIDTH).astype(_BF16), vb3], axis=1)
        kib3 = jnp.concatenate([ki_past.astype(_BF16), kib3], axis=1)
    n_valid = kb3.shape[1]
    s_pad = -(-n_valid // (2 * LANES)) * (2 * LANES)
    if s_pad != n_valid:
        padk = lambda a: jnp.pad(a, ((0, 0), (0, s_pad - n_valid), (0, 0)))
        kb3, vb3, kib3 = padk(kb3), padk(vb3), padk(kib3)
    o_att = _attention(r3(q), r3(qi), r3(wi), r3(gates), kb3, vb3, kib3, q_pos0=pos0, n_valid=n_valid,
                       topk=topk, causal_tiles=k_past is None and t % (2 * LANES) == 0)

    if fused:
        oc, op = mixed
        u3, pin3 = u, pin
    else:
        u3, pin3 = r3(u), r3(pin)
        oc, op = _mixers(u3, ust, pin3, pst, r3(gates), cw, cb, lng, lnb, pw, poolw, pscale,
                         tt=_row_tile(t, 256), q_pos0=pos0)

    tm2 = _row_tile(n, 512)
    x_new = _outproj(x.reshape(n, D_MODEL), o_att.reshape(n, ATT_WIDTH), oc.reshape(n, CONV_WIDTH),
                     op.reshape(n, POOL_WIDTH), p_all.reshape(-1, n, PLE_DIM), layer, w_out, nple, wgate, wproj,
                     nfin, tm=tm2, final=final).reshape(b, t, D_MODEL)

    conv_state = jnp.concatenate([conv_prev, u3], axis=1)[:, -(CONV_K - 1):]
    pool_state = jnp.concatenate([pool_prev, pin3], axis=1)[:, -POOL_HIST:]
    return x_new, (k_all, v_all), r3(ki), conv_state, pool_state


def kernel(x_prompt, x_sample, p_prompt, p_sample, cache_k, cache_v, cache_kidx, state_conv, state_pool, norm_mix, w_in, conv_w, conv_b, conv_ln_g, conv_ln_b, conv_pw, pool_w, pool_scale, w_out, norm_ple, w_ple_gate, w_ple_proj, norm_final):
    bp, tp, _ = x_prompt.shape
    bs, ts, _ = x_sample.shape
    depth = w_in.shape[0]
    past = cache_k.shape[2]
    topk_p = min(TOPK_MAX, tp // 4)
    topk_s = min(TOPK_MAX, (past + ts) // 4)
    conv0 = jnp.zeros((bp, CONV_K - 1, CONV_WIDTH), x_prompt.dtype)
    pool0 = jnp.zeros((bp, POOL_HIST, POOL_WIDTH), x_prompt.dtype)
    nfin = norm_final.reshape(1, D_MODEL)
    hp, hs = x_prompt, x_sample
    kv_p = kv_s = None
    outs_p, outs_s = [], []
    for i in range(depth):
        vec = lambda a: a[i].reshape(1, -1)
        lw = (vec(norm_mix), _split_w_in(w_in[i]), conv_w[i], vec(conv_b), vec(conv_ln_g), vec(conv_ln_b),
              conv_pw[i].astype(_BF16), pool_w[i].astype(_BF16), vec(pool_scale), w_out[i].astype(_BF16),
              vec(norm_ple), w_ple_gate[i].astype(_BF16), w_ple_proj[i].astype(_BF16))
        final = i == depth - 1
        hp, kv_p, *op_ = _layer(hp, p_prompt, kv_p, i, 0, None, None, None, conv0, pool0, topk_p, lw, nfin, final)
        hs, kv_s, *os_ = _layer(hs, p_sample, kv_s, i, past, cache_k[i], cache_v[i], cache_kidx[i],
                                state_conv[i], state_pool[i], topk_s, lw, nfin, final)
        outs_p.append(op_)
        outs_s.append(os_)
    stack = lambda outs, j: jnp.stack([o[j] for o in outs])
    heads = lambda a, b, t: a.reshape(depth, b, t, N_KV_HEADS, HEAD_DIM)
    return ((hp, hs, heads(kv_p[0], bp, tp), heads(kv_p[1], bp, tp)) + tuple(stack(outs_p, j) for j in range(3))
            + (heads(kv_s[0], bs, ts), heads(kv_s[1], bs, ts)) + tuple(stack(outs_s, j) for j in range(3)))
```
